```python
import jax, jax.numpy as jnp
from jax import lax
import numpy as np

D_MODEL = 1024
BATCH = 8
SEQ = 4096
DEPTH = 2

SC_GROUPS = 8
SC_GROUP_DIM = 64
SC_WIDTH = SC_GROUPS * SC_GROUP_DIM
CONV_WIDTH = 3
ATT_HEADS = 8
HEAD_DIM = 64
IDX_HEADS = 4
IDX_DIM = 64
TOPK_MAX = 256
Q_BLOCK = 128
ROT_DIM = HEAD_DIM // 4
ROPE_THETA = 500000.0
IDX_SCALE = (IDX_HEADS * IDX_DIM) ** -0.5
ATT_SCALE = HEAD_DIM ** -0.5
SG_GROUPS = 8
SG_GROUP_DIM = 64
SG_WIDTH = SG_GROUPS * SG_GROUP_DIM
SG_CHUNK = 128
GLA_HEADS = 4
GLA_DK = 64
GLA_DV = 128
GLA_RANK = 16
GLA_CHUNK = 64
GATE_NORMALIZER = 16.0
D_FF = 2816
EPS = 1e-6
LN_EPS = 1e-5

N_EVEN = (DEPTH + 1) // 2
N_ODD = DEPTH // 2
EVEN_SIZES = (3 * SC_WIDTH, ATT_HEADS * HEAD_DIM, HEAD_DIM, HEAD_DIM, IDX_HEADS * IDX_DIM, IDX_DIM, IDX_HEADS)
EVEN_COLS = int(sum(EVEN_SIZES))
EVEN_SPLIT = tuple(int(c) for c in np.cumsum(EVEN_SIZES)[:-1])
EVEN_MIX = SC_WIDTH + ATT_HEADS * HEAD_DIM
ODD_SIZES = (2 * SG_WIDTH, GLA_HEADS * GLA_DK, GLA_HEADS * GLA_DK, GLA_HEADS * GLA_DV, GLA_RANK, GLA_HEADS * GLA_DV)
ODD_COLS = int(sum(ODD_SIZES))
ODD_SPLIT = tuple(int(c) for c in np.cumsum(ODD_SIZES)[:-1])
ODD_MIX = SG_WIDTH + GLA_HEADS * GLA_DV

kernel_name = 'hybrid_conv_dsa_sgu_gla_block'


def rmsnorm(x, g):
    xf = x.astype(jnp.float32)
    y = xf * lax.rsqrt(jnp.mean(xf * xf, axis=-1, keepdims=True) + EPS)
    return (y * g.astype(jnp.float32)).astype(x.dtype)


def layernorm(x, g, b):
    xf = x.astype(jnp.float32)
    mu = jnp.mean(xf, axis=-1, keepdims=True)
    xc = xf - mu
    y = xc * lax.rsqrt(jnp.mean(xc * xc, axis=-1, keepdims=True) + LN_EPS)
    return (y * g.astype(jnp.float32) + b.astype(jnp.float32)).astype(x.dtype)


def causal_dwconv(x, w):
    k_w, c = w.shape
    return lax.conv_general_dilated(x, w[:, None, :].astype(x.dtype), window_strides=(1,),
                                    padding=[(k_w - 1, 0)], dimension_numbers=('NWC', 'WIO', 'NWC'),
                                    feature_group_count=c)


def rope_tables(positions):
    inv_freq = ROPE_THETA ** (-jnp.arange(0, ROT_DIM, 2, dtype=jnp.float32) / ROT_DIM)
    ang = positions.astype(jnp.float32)[..., None] * inv_freq
    return jnp.cos(ang)[:, :, None, :], jnp.sin(ang)[:, :, None, :]


def apply_partial_rope(x, cos, sin):
    half = ROT_DIM // 2
    xf = x.astype(jnp.float32)
    x1 = xf[..., :half]
    x2 = xf[..., half:ROT_DIM]
    out = jnp.concatenate([x1 * cos - x2 * sin, x1 * sin + x2 * cos, xf[..., ROT_DIM:]], axis=-1)
    return out.astype(x.dtype)


def indexed_sparse_attention(q, k, v, iq, ik, iw, topk):
    bsz, s_len = q.shape[0], q.shape[1]
    n_blocks = s_len // Q_BLOCK
    key_pos = jnp.arange(s_len)

    def block(i):
        s0 = i * Q_BLOCK
        qb = lax.dynamic_slice_in_dim(q, s0, Q_BLOCK, axis=1)
        iqb = lax.dynamic_slice_in_dim(iq, s0, Q_BLOCK, axis=1)
        iwb = lax.dynamic_slice_in_dim(iw, s0, Q_BLOCK, axis=1)
        t = s0 + jnp.arange(Q_BLOCK)
        isc = jax.nn.relu(jnp.einsum('bqhd,bsd->bhqs', iqb, ik).astype(jnp.float32))
        isc = jnp.einsum('bhqs,bqh->bqs', isc, iwb.astype(jnp.float32))
        causal = key_pos[None, :] <= t[:, None]
        isc = jnp.where(causal[None], isc, -jnp.inf)
        _, idx = lax.top_k(isc, topk)
        valid = idx <= t[None, :, None]
        ks = jax.vmap(lambda kk, ii: kk[ii])(k, idx)
        vs = jax.vmap(lambda vv, ii: vv[ii])(v, idx)
        sc = jnp.einsum('bqhd,bqkd->bhqk', qb, ks).astype(jnp.float32) * ATT_SCALE
        sc = jnp.where(valid[:, None], sc, -jnp.inf)
        p = jax.nn.softmax(sc, axis=-1).astype(v.dtype)
        return jnp.einsum('bhqk,bqkd->bqhd', p, vs)

    out = lax.map(block, jnp.arange(n_blocks))
    return out.transpose(1, 0, 2, 3, 4).reshape(bsz, s_len, ATT_HEADS * HEAD_DIM)


def gla_chunked(q, k, v, g):
    bsz, s_len, h, dk = q.shape
    dv = v.shape[-1]
    nc = s_len // GLA_CHUNK

    def to_chunks(t):
        return t.reshape(bsz, nc, GLA_CHUNK, h, t.shape[-1]).transpose(1, 0, 3, 2, 4)

    causal = jnp.tril(jnp.ones((GLA_CHUNK, GLA_CHUNK), dtype=bool))

    def step(state, inp):
        qc, kc, vc, gc = inp
        b = jnp.cumsum(gc, axis=2)
        b_last = b[:, :, -1:, :]
        o_inter = jnp.einsum('bhcd,bhde->bhce', qc * jnp.exp(b), state)
        diff = b[:, :, :, None, :] - b[:, :, None, :, :]
        decay = jnp.exp(jnp.where(causal[:, :, None], diff, -jnp.inf))
        a = jnp.einsum('bhid,bhjd,bhijd->bhij', qc, kc, decay)
        o = o_inter + jnp.einsum('bhij,bhje->bhie', a, vc)
        state = state * jnp.exp(b_last)[:, :, 0, :, None] + jnp.einsum('bhcd,bhce->bhde', kc * jnp.exp(b_last - b), vc)
        return state, o

    s0 = jnp.zeros((bsz, h, dk, dv), jnp.float32)
    _, o = lax.scan(step, s0, (to_chunks(q), to_chunks(k), to_chunks(v), to_chunks(g)))
    return o.transpose(1, 0, 3, 2, 4).reshape(bsz, s_len, h, dv)


def even_mixer(h, cos, sin, w_in, conv_w, q_g, k_g, w_out):
    bsz, s_len, _ = h.shape
    p = h @ w_in
    sc, q, k, v, iq, ik, iw = jnp.split(p, EVEN_SPLIT, axis=-1)
    bg, cg, hv = jnp.split(sc, 3, axis=-1)
    ya = bg * causal_dwconv(cg * hv, conv_w)
    q = apply_partial_rope(rmsnorm(q.reshape(bsz, s_len, ATT_HEADS, HEAD_DIM), q_g), cos, sin)
    k = apply_partial_rope(rmsnorm(k.reshape(bsz, s_len, 1, HEAD_DIM), k_g), cos, sin)[:, :, 0]
    iq = apply_partial_rope(iq.reshape(bsz, s_len, IDX_HEADS, IDX_DIM), cos, sin)
    ik = apply_partial_rope(ik.reshape(bsz, s_len, 1, IDX_DIM), cos, sin)[:, :, 0]
    topk = min(TOPK_MAX, s_len // 4)
    yb = indexed_sparse_attention(q, k, v, iq, ik, iw * IDX_SCALE, topk)
    return jnp.concatenate([ya, yb], axis=-1) @ w_out


def odd_mixer(h, w_in, ln_g, ln_b, sg_w, sg_b, w_gate, b_gate, o_norm, w_out):
    bsz, s_len, _ = h.shape
    p = h @ w_in
    uv, q, k, v, glr, r = jnp.split(p, ODD_SPLIT, axis=-1)
    u, vv = jnp.split(jax.nn.gelu(uv, approximate=False), 2, axis=-1)
    vv = layernorm(vv, ln_g, ln_b).reshape(bsz, s_len // SG_CHUNK, SG_CHUNK, SG_GROUPS, SG_GROUP_DIM)
    w_mask = jnp.tril(sg_w)
    sp = jnp.einsum('gts,bcsgd->bctgd', w_mask, vv) + sg_b.T[None, None, :, :, None]
    yc = u * sp.reshape(bsz, s_len, SG_WIDTH)
    g = jax.nn.log_sigmoid((glr @ w_gate + b_gate).astype(jnp.float32)) / GATE_NORMALIZER
    qf = q.reshape(bsz, s_len, GLA_HEADS, GLA_DK).astype(jnp.float32) * GLA_DK ** -0.5
    kf = k.reshape(bsz, s_len, GLA_HEADS, GLA_DK).astype(jnp.float32)
    vf = v.reshape(bsz, s_len, GLA_HEADS, GLA_DV).astype(jnp.float32)
    gf = g.reshape(bsz, s_len, GLA_HEADS, GLA_DK)
    o = gla_chunked(qf, kf, vf, gf).astype(h.dtype)
    o = rmsnorm(o, o_norm) * jax.nn.silu(r.reshape(bsz, s_len, GLA_HEADS, GLA_DV))
    yd = o.reshape(bsz, s_len, GLA_HEADS * GLA_DV)
    return jnp.concatenate([yc, yd], axis=-1) @ w_out


def conv_ffn(h, w_up, conv_w, conv_b, w_down):
    u = causal_dwconv(h @ w_up, conv_w) + conv_b
    gate, up = jnp.split(u, 2, axis=-1)
    return (jax.nn.silu(gate) * up) @ w_down


def setup_inputs(seed: int = 0) -> dict:
    key = jax.random.key(seed)
    ks = iter(jax.random.split(key, 32))

    def nrm(shape, scale):
        return jax.random.normal(next(ks), shape, jnp.float32) * scale

    def gain(shape):
        return 1.0 + nrm(shape, 0.1)

    x = nrm((BATCH, SEQ, D_MODEL), 1.0)
    offsets = jax.random.randint(next(ks), (BATCH, 1), 0, 1024, dtype=jnp.int32)
    positions = jnp.arange(SEQ, dtype=jnp.int32)[None, :] + offsets
    return {
        'x': x,
        'positions': positions,
        'ev_norm': gain((N_EVEN, D_MODEL)),
        'ev_w_in': nrm((N_EVEN, D_MODEL, EVEN_COLS), D_MODEL ** -0.5),
        'sc_conv_w': nrm((N_EVEN, CONV_WIDTH, SC_WIDTH), CONV_WIDTH ** -0.5),
        'q_norm': gain((N_EVEN, HEAD_DIM)),
        'k_norm': gain((N_EVEN, HEAD_DIM)),
        'ev_w_out': nrm((N_EVEN, EVEN_MIX, D_MODEL), EVEN_MIX ** -0.5),
        'od_norm': gain((N_ODD, D_MODEL)),
        'od_w_in': nrm((N_ODD, D_MODEL, ODD_COLS), D_MODEL ** -0.5),
        'sg_ln_g': gain((N_ODD, SG_WIDTH)),
        'sg_ln_b': nrm((N_ODD, SG_WIDTH), 0.1),
        'sg_w': nrm((N_ODD, SG_GROUPS, SG_CHUNK, SG_CHUNK), SG_CHUNK ** -0.5),
        'sg_b': gain((N_ODD, SG_GROUPS, SG_CHUNK)),
        'gla_w_gate': nrm((N_ODD, GLA_RANK, GLA_HEADS * GLA_DK), GLA_RANK ** -0.5),
        'gla_b_gate': nrm((N_ODD, GLA_HEADS * GLA_DK), 0.1),
        'gla_o_norm': gain((N_ODD, GLA_DV)),
        'od_w_out': nrm((N_ODD, ODD_MIX, D_MODEL), ODD_MIX ** -0.5),
        'ffn_norm': gain((DEPTH, D_MODEL)),
        'ffn_w_up': nrm((DEPTH, D_MODEL, 2 * D_FF), D_MODEL ** -0.5),
        'ffn_conv_w': nrm((DEPTH, CONV_WIDTH, 2 * D_FF), CONV_WIDTH ** -0.5),
        'ffn_conv_b': nrm((DEPTH, 2 * D_FF), 0.01),
        'ffn_w_down': nrm((DEPTH, D_FF, D_MODEL), D_FF ** -0.5),
    }


def reference(x, positions, ev_norm, ev_w_in, sc_conv_w, q_norm, k_norm, ev_w_out,
              od_norm, od_w_in, sg_ln_g, sg_ln_b, sg_w, sg_b, gla_w_gate, gla_b_gate, gla_o_norm, od_w_out,
              ffn_norm, ffn_w_up, ffn_conv_w, ffn_conv_b, ffn_w_down):
    cos, sin = rope_tables(positions)
    for layer in range(DEPTH):
        i = layer // 2
        if layer % 2 == 0:
            x = x + even_mixer(rmsnorm(x, ev_norm[i]), cos, sin, ev_w_in[i], sc_conv_w[i],
                               q_norm[i], k_norm[i], ev_w_out[i])
        else:
            x = x + odd_mixer(rmsnorm(x, od_norm[i]), od_w_in[i], sg_ln_g[i], sg_ln_b[i], sg_w[i], sg_b[i],
                              gla_w_gate[i], gla_b_gate[i], gla_o_norm[i], od_w_out[i])
        x = x + conv_ffn(rmsnorm(x, ffn_norm[layer]), ffn_w_up[layer], ffn_conv_w[layer],
                         ffn_conv_b[layer], ffn_w_down[layer])
    return x
```

```python
import functools

import numpy as np
import jax
import jax.numpy as jnp
from jax import lax
from jax.experimental import pallas as pl
from jax.experimental.pallas import tpu as pltpu

F32 = jnp.float32
BF16 = jnp.bfloat16

D_MODEL = 1024
SC_WIDTH = 512
CONV_WIDTH = 3
ATT_HEADS = 8
HEAD_DIM = 64
IDX_HEADS = 4
IDX_DIM = 64
TOPK_MAX = 256
ROT_DIM = HEAD_DIM // 4
ROT_HALF = ROT_DIM // 2
ROPE_THETA = 500000.0
IDX_SCALE = (IDX_HEADS * IDX_DIM) ** -0.5
ATT_SCALE = HEAD_DIM ** -0.5
SG_GROUPS = 8
SG_GROUP_DIM = 64
SG_WIDTH = 512
SG_CHUNK = 128
GLA_HEADS = 4
GLA_DK = 64
GLA_DV = 128
GLA_RANK = 16
GLA_CHUNK = 64
GATE_NORMALIZER = 16.0
D_FF = 2816
EPS = 1e-6
LN_EPS = 1e-5

EVEN_COLS = 2500
EVEN_PAD = 2560
ODD_PAD = 2688

LANES = 128
SUBLANES = 8
VMEM_LIMIT = 56 * 1024 * 1024

TOK_TILE = 512
ATT_TILE = 256
ROW_STRIP = 64
BISECT_STEPS = 14
FF_CHUNK = 256
NEG_BIG = -1e30


def _const_spec(arr):
    nd = arr.ndim
    return pl.BlockSpec(arr.shape, lambda *_: (0,) * nd, pipeline_mode=pl.Buffered(1))


def _dot(a, b):
    return jnp.dot(a, b, preferred_element_type=F32)


def _dot_nt(a, b):
    return lax.dot_general(a, b, (((1,), (1,)), ((), ())), preferred_element_type=F32)


def _split_bf16(a):
    hi = a.astype(BF16)
    lo = (a - hi.astype(F32)).astype(BF16)
    return hi, lo


def _dot_hilo(a, b):
    hi, lo = _split_bf16(a)
    return _dot(hi, b) + _dot(lo, b)


def _dot_hilo_left(b, a):
    hi, lo = _split_bf16(a)
    return _dot(b, hi) + _dot(b, lo)


def _rmsnorm_rows(x, g):
    return x * lax.rsqrt(jnp.mean(x * x, axis=-1, keepdims=True) + EPS) * g


def _rope_kernel(invf_ref, pos_ref, cos_ref, sin_ref):
    pos = pos_ref[...]
    for f in range(ROT_HALF):
        ang = pos * invf_ref[f]
        cos_ref[f] = jnp.cos(ang)
        sin_ref[f] = jnp.sin(ang)


def _rope_tables(positions):
    bsz, s_len = positions.shape
    inv_freq = ROPE_THETA ** (-jnp.arange(0, ROT_DIM, 2, dtype=F32) / ROT_DIM)
    cos, sin = pl.pallas_call(
        _rope_kernel,
        out_shape=[jax.ShapeDtypeStruct((ROT_HALF, bsz, s_len), F32)] * 2,
        in_specs=[pl.BlockSpec(memory_space=pltpu.SMEM), pl.BlockSpec(memory_space=pltpu.VMEM)],
        out_specs=[pl.BlockSpec(memory_space=pltpu.VMEM)] * 2,
        name="rope_tables",
    )(inv_freq, positions.astype(F32))
    return jnp.concatenate([cos, sin], axis=0).transpose(1, 2, 0)


def _rope_expand_consts():
    expand = np.zeros((2 * ROT_HALF, 4 * LANES), np.float32)
    add = np.zeros((1, 4 * LANES), np.float32)
    for l in range(LANES):
        m = l % HEAD_DIM
        if m < ROT_HALF:
            expand[m, l] = 1.0
            expand[ROT_HALF + m, LANES + l] = -1.0
        elif m < ROT_DIM:
            expand[m - ROT_HALF, l] = 1.0
            expand[m, LANES + l] = 1.0
        else:
            add[0, l] = 1.0
        if l < HEAD_DIM:
            expand[:, 2 * LANES + l] = expand[:, l]
            expand[:, 3 * LANES + l] = expand[:, LANES + l]
            add[0, 2 * LANES + l] = add[0, l]
        else:
            add[0, 2 * LANES + l] = 1.0
    return expand, add


def _rope_apply(x, cos, sin_signed, first_half):
    width = x.shape[1]
    partner = jnp.where(first_half, pltpu.roll(x, width - ROT_HALF, 1), pltpu.roll(x, ROT_HALF, 1))
    return x * cos + partner * sin_signed


def _even_in_kernel(x_ref, g_ref, w_ref, cw_ref, qg_ref, kg_ref, cs_ref, ex_ref, exadd_ref, bdq_ref, bdk_ref,
                    ya_ref, q_ref, k_ref, v_ref, iq_ref, ik_ref, iw_ref, zbuf):
    tq = x_ref.shape[1]

    @pl.when(pl.program_id(1) == 0)
    def _():
        zbuf[0:SUBLANES, :] = jnp.zeros((SUBLANES, SC_WIDTH), F32)

    h = _rmsnorm_rows(x_ref[0], g_ref[...]).astype(BF16)

    p_sc = _dot(h, w_ref[:, 0:3 * SC_WIDTH])
    bg = p_sc[:, 0:SC_WIDTH]
    z = p_sc[:, SC_WIDTH:2 * SC_WIDTH] * p_sc[:, 2 * SC_WIDTH:3 * SC_WIDTH]
    zbuf[SUBLANES:SUBLANES + tq, :] = z
    cw = cw_ref[...]
    conv = (cw[2:3, :] * z + cw[1:2, :] * zbuf[SUBLANES - 1:SUBLANES - 1 + tq, :]
            + cw[0:1, :] * zbuf[SUBLANES - 2:SUBLANES - 2 + tq, :])
    ya_ref[0] = (bg * conv).astype(BF16)
    zbuf[0:SUBLANES, :] = zbuf[tq:tq + SUBLANES, :]

    tab = _dot_hilo(cs_ref[0], ex_ref[...]) + exadd_ref[...]
    cos_p, sin_p = tab[:, 0:LANES], tab[:, LANES:2 * LANES]
    cos_k, sin_k = tab[:, 2 * LANES:3 * LANES], tab[:, 3 * LANES:4 * LANES]
    lane = lax.broadcasted_iota(jnp.int32, (tq, LANES), 1)
    first_half = (lane % HEAD_DIM) < ROT_HALF

    c0 = 3 * SC_WIDTH
    pq = _dot(h, w_ref[:, c0:c0 + ATT_HEADS * HEAD_DIM])
    ms = _dot_hilo(pq * pq, bdq_ref[...])
    qn = pq * lax.rsqrt(ms + EPS) * qg_ref[...]
    qr = _rope_apply(qn, jnp.concatenate([cos_p] * 4, axis=1), jnp.concatenate([sin_p] * 4, axis=1),
                     jnp.concatenate([first_half] * 4, axis=1)) * ATT_SCALE
    for hh in range(ATT_HEADS):
        q_ref[0, hh] = qr[:, hh * HEAD_DIM:(hh + 1) * HEAD_DIM].astype(BF16)

    c0 += ATT_HEADS * HEAD_DIM
    pkv = _dot(h, w_ref[:, c0:c0 + LANES])
    ms_k = _dot_hilo(pkv * pkv, bdk_ref[...])
    is_key = lane < HEAD_DIM
    kvn = pkv * jnp.where(is_key, lax.rsqrt(ms_k + EPS) * kg_ref[...], 1.0)
    kvr = _rope_apply(kvn, cos_k, sin_k, first_half)
    k_ref[0] = kvr[:, 0:HEAD_DIM].astype(BF16)
    v_ref[0] = kvr[:, HEAD_DIM:LANES].astype(BF16)

    c0 += LANES
    piq = _dot(h, w_ref[:, c0:c0 + IDX_HEADS * IDX_DIM])
    iqr = _rope_apply(piq, jnp.concatenate([cos_p] * 2, axis=1), jnp.concatenate([sin_p] * 2, axis=1),
                      jnp.concatenate([first_half] * 2, axis=1))
    for hh in range(IDX_HEADS):
        iq_ref[0, hh] = iqr[:, hh * IDX_DIM:(hh + 1) * IDX_DIM].astype(BF16)

    c0 += IDX_HEADS * IDX_DIM
    pik = _dot(h, w_ref[:, c0:c0 + LANES])
    ikr = _rope_apply(pik, cos_k, sin_k, first_half)
    ik_ref[0] = ikr[:, 0:IDX_DIM].astype(BF16)
    iw_ref[0] = ikr[:, IDX_DIM:IDX_DIM + IDX_HEADS] * IDX_SCALE


def _even_in_proj(x, cs, norm_g, w_in, conv_w, q_g, k_g):
    bsz, s_len, _ = x.shape
    tq = min(TOK_TILE, s_len)
    w = jnp.pad(w_in, ((0, 0), (0, EVEN_PAD - EVEN_COLS))).astype(BF16)
    expand, add = _rope_expand_consts()
    hd = np.arange(ATT_HEADS * HEAD_DIM) // HEAD_DIM
    bdq = jnp.asarray((hd[:, None] == hd[None, :]) / HEAD_DIM, BF16)
    kk = np.arange(LANES)
    bdk = jnp.asarray(((kk[:, None] < HEAD_DIM) & (kk[None, :] < HEAD_DIM)) / HEAD_DIM, BF16)
    consts = [
        norm_g.reshape(1, D_MODEL), w, conv_w,
        jnp.tile(q_g, ATT_HEADS).reshape(1, -1),
        jnp.concatenate([k_g, jnp.ones((LANES - HEAD_DIM,), F32)]).reshape(1, LANES),
    ]
    consts2 = [jnp.asarray(expand, BF16), jnp.asarray(add, F32), bdq, bdk]
    tok = lambda width: pl.BlockSpec((1, tq, width), lambda b, j: (b, j, 0))
    heads = lambda n: pl.BlockSpec((1, n, tq, HEAD_DIM), lambda b, j: (b, 0, j, 0))
    return pl.pallas_call(
        _even_in_kernel,
        grid=(bsz, s_len // tq),
        in_specs=[tok(D_MODEL)] + [_const_spec(c) for c in consts] + [tok(2 * ROT_HALF)]
                 + [_const_spec(c) for c in consts2],
        out_specs=[tok(SC_WIDTH), heads(ATT_HEADS), tok(HEAD_DIM), tok(HEAD_DIM), heads(IDX_HEADS),
                   tok(IDX_DIM), tok(IDX_HEADS)],
        out_shape=[
            jax.ShapeDtypeStruct((bsz, s_len, SC_WIDTH), BF16),
            jax.ShapeDtypeStruct((bsz, ATT_HEADS, s_len, HEAD_DIM), BF16),
            jax.ShapeDtypeStruct((bsz, s_len, HEAD_DIM), BF16),
            jax.ShapeDtypeStruct((bsz, s_len, HEAD_DIM), BF16),
            jax.ShapeDtypeStruct((bsz, IDX_HEADS, s_len, IDX_DIM), BF16),
            jax.ShapeDtypeStruct((bsz, s_len, IDX_DIM), BF16),
            jax.ShapeDtypeStruct((bsz, s_len, IDX_HEADS), F32),
        ],
        scratch_shapes=[pltpu.VMEM((tq + SUBLANES, SC_WIDTH), F32)],
        compiler_params=pltpu.CompilerParams(dimension_semantics=("arbitrary", "arbitrary"),
                                             vmem_limit_bytes=VMEM_LIMIT),
        name="even_in_proj",
    )(x, *consts, cs, *consts2)


def _fold(x):
    return x[:, 0:LANES] + x[:, LANES:2 * LANES]


def _twice(x):
    return jnp.concatenate([x, x], axis=1)


def _dsa_kernel(iq_ref, iw_ref, q_ref, ik_ref, k_ref, v_ref, tri_ref, ones2_ref, ones_ref, out_ref,
                sc_ref, lo_ref, hi_ref, thr_ref, done_ref, t_ref, r_ref, tie_ref, m_ref, l_ref, acc_ref,
                *, topk, s_len):
    tq = ATT_TILE
    i = pl.program_id(1)
    nk = i + 1
    n_strips = tq // ROW_STRIP
    row = i * tq + lax.broadcasted_iota(jnp.int32, (tq, LANES), 0)
    keff = jnp.minimum(row + 1, topk).astype(F32)
    ones = ones_ref[...]

    iw = iw_ref[0]
    lo_ref[...] = jnp.full((tq, LANES), jnp.inf, F32)
    hi_ref[...] = jnp.full((tq, LANES), -jnp.inf, F32)

    def score_body(c, carry):
        off = pl.multiple_of(c * tq, tq)
        ikc = ik_ref[0, pl.ds(off, tq), :]
        acc = jnp.zeros((tq, tq), F32)
        for hh in range(IDX_HEADS):
            z = _dot_nt(iq_ref[0, hh], ikc)
            acc = acc + jnp.maximum(z, 0.0) * iw[:, hh:hh + 1]
        key = off + lax.broadcasted_iota(jnp.int32, (tq, tq), 1)
        valid = key <= _twice(row)
        sc_ref[c] = jnp.where(valid, acc, -jnp.inf)
        lo_c = jnp.where(valid, acc, jnp.inf)
        hi_c = jnp.where(valid, acc, -jnp.inf)
        lo_ref[...] = jnp.minimum(lo_ref[...], jnp.minimum(lo_c[:, 0:LANES], lo_c[:, LANES:]))
        hi_ref[...] = jnp.maximum(hi_ref[...], jnp.maximum(hi_c[:, 0:LANES], hi_c[:, LANES:]))
        return carry

    lax.fori_loop(0, nk, score_body, 0)
    lo_ref[...] = jnp.broadcast_to(jnp.min(lo_ref[...], axis=1, keepdims=True), (tq, LANES))
    hi_ref[...] = jnp.broadcast_to(jnp.max(hi_ref[...], axis=1, keepdims=True), (tq, LANES))

    def count_ge(thr_ref_):
        accs = []
        for s in range(n_strips):
            rows = slice(s * ROW_STRIP, (s + 1) * ROW_STRIP)
            thr2 = _twice(thr_ref_[rows, :])

            def body(c, acc, rows=rows, thr2=thr2):
                return acc + _fold(jnp.where(sc_ref[c, rows, :] >= thr2, 1.0, 0.0))

            accs.append(lax.fori_loop(0, nk, body, jnp.zeros((ROW_STRIP, LANES), F32)))
        return _dot(jnp.concatenate(accs, axis=0).astype(BF16), ones)

    def bisect_body(_, carry):
        lo = lo_ref[...]
        hi = hi_ref[...]
        mid = 0.5 * lo + 0.5 * hi
        thr_ref[...] = mid
        ge = count_ge(thr_ref) >= keff
        lo_ref[...] = jnp.where(ge, mid, lo)
        hi_ref[...] = jnp.where(ge, hi, mid)
        return carry

    lax.fori_loop(0, BISECT_STEPS, bisect_body, 0)

    thr_ref[...] = hi_ref[...]
    done_ref[...] = jnp.zeros((tq, LANES), F32)
    t_ref[...] = jnp.zeros((tq, LANES), F32)
    r_ref[...] = jnp.zeros((tq, LANES), F32)

    def snap_cond(state):
        it, pending = state
        return jnp.logical_and(pending > 0, it < s_len + 2)

    def snap_body(state):
        it, _ = state
        ges, gts, nxts = [], [], []
        for s in range(n_strips):
            rows = slice(s * ROW_STRIP, (s + 1) * ROW_STRIP)
            thr2 = _twice(thr_ref[rows, :])

            def body(c, acc, rows=rows, thr2=thr2):
                a_ge, a_gt, a_nx = acc
                x = sc_ref[c, rows, :]
                ge = x >= thr2
                below = jnp.where(ge, -jnp.inf, x)
                return (a_ge + _fold(jnp.where(ge, 1.0, 0.0)),
                        a_gt + _fold(jnp.where(x > thr2, 1.0, 0.0)),
                        jnp.maximum(a_nx, jnp.maximum(below[:, 0:LANES], below[:, LANES:])))

            zero = jnp.zeros((ROW_STRIP, LANES), F32)
            a_ge, a_gt, a_nx = lax.fori_loop(0, nk, body, (zero, zero, jnp.full((ROW_STRIP, LANES), -jnp.inf, F32)))
            ges.append(a_ge)
            gts.append(a_gt)
            nxts.append(a_nx)
        c_ge = _dot(jnp.concatenate(ges, axis=0).astype(BF16), ones)
        c_gt = _dot(jnp.concatenate(gts, axis=0).astype(BF16), ones)
        nxt = jnp.broadcast_to(jnp.max(jnp.concatenate(nxts, axis=0), axis=1, keepdims=True), (tq, LANES))
        done = done_ref[...]
        v = thr_ref[...]
        newly = jnp.logical_and(done < 0.5, c_ge >= keff)
        t_ref[...] = jnp.where(newly, v, t_ref[...])
        r_ref[...] = jnp.where(newly, keff - c_gt, r_ref[...])
        done = jnp.where(newly, 1.0, done)
        done_ref[...] = done
        thr_ref[...] = jnp.where(done > 0.5, v, nxt)
        return it + 1, (jnp.min(done) < 0.5).astype(jnp.int32)

    lax.while_loop(snap_cond, snap_body, (jnp.int32(0), jnp.int32(1)))

    tie_ref[...] = jnp.zeros((tq, LANES), F32)
    m_ref[...] = jnp.full(m_ref.shape, NEG_BIG, F32)
    l_ref[...] = jnp.zeros(l_ref.shape, F32)
    acc_ref[...] = jnp.zeros(acc_ref.shape, F32)
    t2 = _twice(t_ref[...])
    r2 = _twice(r_ref[...])

    def att_body(c, carry):
        off = pl.multiple_of(c * tq, tq)
        x = sc_ref[c]
        eq = x == t2
        eqb = jnp.where(eq, 1.0, 0.0).astype(BF16)
        prefix = _dot(eqb, tri_ref[...]) + _twice(tie_ref[...])
        tie_ref[...] = tie_ref[...] + _dot(eqb, ones2_ref[...])
        sel = jnp.logical_or(x > t2, jnp.logical_and(eq, prefix <= r2))
        bias = jnp.where(sel, 0.0, NEG_BIG)
        kc = k_ref[0, pl.ds(off, tq), :]
        vc = v_ref[0, pl.ds(off, tq), :]
        for hh in range(ATT_HEADS):
            s = _dot_nt(q_ref[0, hh], kc) + bias
            m_old = m_ref[hh]
            m_new = jnp.maximum(m_old, jnp.max(s, axis=1, keepdims=True))
            alpha = jnp.exp(m_old - m_new)
            p = jnp.exp(s - _twice(m_new))
            m_ref[hh] = m_new
            l_ref[hh] = alpha * l_ref[hh] + _fold(p)
            acc_ref[hh] = alpha[:, 0:HEAD_DIM] * acc_ref[hh] + _dot(p.astype(BF16), vc)
        return carry

    lax.fori_loop(0, nk, att_body, 0)
    outs = []
    for hh in range(ATT_HEADS):
        outs.append(acc_ref[hh] / jnp.sum(l_ref[hh], axis=1, keepdims=True))
    out_ref[0] = jnp.concatenate(outs, axis=1).astype(BF16)


def _sparse_attention(q, k, v, iq, ik, iw):
    bsz, _, s_len, _ = q.shape
    tq = ATT_TILE
    topk = min(TOPK_MAX, s_len // 4)
    n_chunks = s_len // tq
    idx = np.arange(tq)
    tri = jnp.asarray(idx[:, None] <= idx[None, :], BF16)
    ones2 = jnp.ones((tq, LANES), BF16)
    ones = jnp.ones((LANES, LANES), BF16)
    heads = lambda n: pl.BlockSpec((1, n, tq, HEAD_DIM), lambda b, j: (b, 0, j, 0))
    full = pl.BlockSpec((1, s_len, HEAD_DIM), lambda b, j: (b, 0, 0))
    rep = lambda: pltpu.VMEM((tq, LANES), F32)
    return pl.pallas_call(
        functools.partial(_dsa_kernel, topk=topk, s_len=s_len),
        grid=(bsz, n_chunks),
        in_specs=[heads(IDX_HEADS), pl.BlockSpec((1, tq, IDX_HEADS), lambda b, j: (b, j, 0)), heads(ATT_HEADS),
                  full, full, full, _const_spec(tri), _const_spec(ones2), _const_spec(ones)],
        out_specs=pl.BlockSpec((1, tq, ATT_HEADS * HEAD_DIM), lambda b, j: (b, j, 0)),
        out_shape=jax.ShapeDtypeStruct((bsz, s_len, ATT_HEADS * HEAD_DIM), BF16),
        scratch_shapes=[
            pltpu.VMEM((n_chunks, tq, tq), F32),
            rep(), rep(), rep(), rep(), rep(), rep(), rep(),
            pltpu.VMEM((ATT_HEADS, tq, LANES), F32),
            pltpu.VMEM((ATT_HEADS, tq, LANES), F32),
            pltpu.VMEM((ATT_HEADS, tq, HEAD_DIM), F32),
        ],
        compiler_params=pltpu.CompilerParams(dimension_semantics=("arbitrary", "arbitrary"),
                                             vmem_limit_bytes=VMEM_LIMIT),
        name="sparse_attention",
    )(iq, iw, q, ik, k, v, tri, ones2, ones)


def _mix_ffn_kernel(x_ref, ya_ref, yb_ref, wo_ref, fg_ref, wg_ref, wu_ref, wd_ref, cwg_ref, cwu_ref, cbg_ref,
                    cbu_ref, out_ref, h_ref, acc_ref, gbuf, ubuf, carry_ref):
    tq = x_ref.shape[1]
    half = ya_ref.shape[2]
    n_ff = wg_ref.shape[0]

    @pl.when(pl.program_id(1) == 0)
    def _():
        carry_ref[...] = jnp.zeros(carry_ref.shape, F32)

    x1 = x_ref[0] + _dot(ya_ref[0], wo_ref[0:half, :]) + _dot(yb_ref[0], wo_ref[half:2 * half, :])
    acc_ref[...] = x1
    h_ref[...] = _rmsnorm_rows(x1, fg_ref[...]).astype(BF16)

    def conv(u, buf, c, slot, cw, cb):
        buf[0:SUBLANES, :] = carry_ref[c, slot]
        buf[SUBLANES:SUBLANES + tq, :] = u
        y = (cw[2:3, :] * u + cw[1:2, :] * buf[SUBLANES - 1:SUBLANES - 1 + tq, :]
             + cw[0:1, :] * buf[SUBLANES - 2:SUBLANES - 2 + tq, :] + cb)
        carry_ref[c, slot] = buf[tq:tq + SUBLANES, :]
        return y

    def body(c, carry):
        h = h_ref[...]
        gate = conv(_dot(h, wg_ref[c]), gbuf, c, 0, cwg_ref[c], cbg_ref[c])
        up = conv(_dot(h, wu_ref[c]), ubuf, c, 1, cwu_ref[c], cbu_ref[c])
        act = gate * jax.nn.sigmoid(gate) * up
        acc_ref[...] += _dot(act.astype(BF16), wd_ref[c])
        return carry

    lax.fori_loop(0, n_ff, body, 0)
    out_ref[0] = acc_ref[...]


def _mix_ffn(x, ya, yb, w_out, ffn_g, w_up, conv_w, conv_b, w_down):
    bsz, s_len, _ = x.shape
    tq = min(TOK_TILE, s_len)
    half = ya.shape[2]
    n_ff = D_FF // FF_CHUNK
    chunked = lambda a: a.reshape(a.shape[0], n_ff, FF_CHUNK).transpose(1, 0, 2)
    consts = [
        w_out.astype(BF16), ffn_g.reshape(1, D_MODEL),
        chunked(w_up[:, :D_FF]).astype(BF16), chunked(w_up[:, D_FF:]).astype(BF16),
        w_down.reshape(n_ff, FF_CHUNK, D_MODEL).astype(BF16),
        chunked(conv_w[:, :D_FF]), chunked(conv_w[:, D_FF:]),
        chunked(conv_b[None, :D_FF]), chunked(conv_b[None, D_FF:]),
    ]
    tok = lambda width: pl.BlockSpec((1, tq, width), lambda b, j: (b, j, 0))
    return pl.pallas_call(
        _mix_ffn_kernel,
        grid=(bsz, s_len // tq),
        in_specs=[tok(D_MODEL), tok(half), tok(half)] + [_const_spec(c) for c in consts],
        out_specs=tok(D_MODEL),
        out_shape=jax.ShapeDtypeStruct((bsz, s_len, D_MODEL), F32),
        scratch_shapes=[
            pltpu.VMEM((tq, D_MODEL), BF16),
            pltpu.VMEM((tq, D_MODEL), F32),
            pltpu.VMEM((tq + SUBLANES, FF_CHUNK), F32),
            pltpu.VMEM((tq + SUBLANES, FF_CHUNK), F32),
            pltpu.VMEM((n_ff, 2, SUBLANES, FF_CHUNK), F32),
        ],
        compiler_params=pltpu.CompilerParams(dimension_semantics=("arbitrary", "arbitrary"),
                                             vmem_limit_bytes=VMEM_LIMIT),
        name="mix_out_ffn",
    )(x, ya, yb, *consts)


def _odd_in_kernel(x_ref, g_ref, w_ref, lng_ref, lnb_ref, sgw_ref, sgb_ref, wgate_ref, bgate_ref, onorm_ref,
                   lblk_ref, oblk_ref, yc_ref, yd_ref, state_ref, obuf):
    tq = x_ref.shape[1]
    hk = GLA_HEADS * GLA_DK

    @pl.when(pl.program_id(1) == 0)
    def _():
        state_ref[...] = jnp.zeros(state_ref.shape, F32)

    h = _rmsnorm_rows(x_ref[0], g_ref[...]).astype(BF16)

    uv = _dot(h, w_ref[:, 0:2 * SG_WIDTH])
    gel = 0.5 * uv * (1.0 + lax.erf(uv * np.float32(np.sqrt(0.5))))
    u = gel[:, 0:SG_WIDTH]
    vv = gel[:, SG_WIDTH:2 * SG_WIDTH]
    mu = jnp.mean(vv, axis=-1, keepdims=True)
    xc = vv - mu
    vn = (xc * lax.rsqrt(jnp.mean(xc * xc, axis=-1, keepdims=True) + LN_EPS) * lng_ref[...] + lnb_ref[...])
    vn = vn.astype(BF16)
    low = lax.broadcasted_iota(jnp.int32, (SG_CHUNK, LANES), 1) < SG_GROUP_DIM
    zero = jnp.zeros((SG_CHUNK, LANES), BF16)
    for cc in range(tq // SG_CHUNK):
        rows = slice(cc * SG_CHUNK, (cc + 1) * SG_CHUNK)
        for p in range(SG_WIDTH // LANES):
            cols = slice(p * LANES, (p + 1) * LANES)
            tile = vn[rows, cols]
            rhs = jnp.concatenate([jnp.where(low, tile, zero), jnp.where(low, zero, tile)], axis=0)
            sp = _dot(sgw_ref[p], rhs) + sgb_ref[:, cols]
            yc_ref[0, rows, cols] = (u[rows, cols] * sp).astype(BF16)

    c0 = 2 * SG_WIDTH
    qf = _dot(h, w_ref[:, c0:c0 + hk]) * (GLA_DK ** -0.5)
    kf = _dot(h, w_ref[:, c0 + hk:c0 + 2 * hk])
    c0 += 2 * hk
    vf = _dot(h, w_ref[:, c0:c0 + GLA_HEADS * GLA_DV]).astype(BF16)
    c0 += GLA_HEADS * GLA_DV
    rr = _dot(h, w_ref[:, c0:c0 + GLA_HEADS * GLA_DV])
    c0 += GLA_HEADS * GLA_DV
    pg = _dot(h, w_ref[:, c0:c0 + LANES])
    g = jax.nn.log_sigmoid(_dot(pg.astype(BF16), wgate_ref[...]) + bgate_ref[...]) * (1.0 / GATE_NORMALIZER)
    b = _dot_hilo_left(lblk_ref[...], g)
    gtot = _dot_hilo_left(oblk_ref[...], g)
    qp = qf * jnp.exp(b)
    kp = (kf * jnp.exp(-b)).astype(BF16)
    kpp = kf * jnp.exp(gtot - b)
    dec = jnp.exp(gtot)

    lane_head = lax.broadcasted_iota(jnp.int32, (GLA_CHUNK, hk), 1) // GLA_DK
    rr_i = lax.broadcasted_iota(jnp.int32, (GLA_HEADS * GLA_CHUNK, GLA_CHUNK), 0) % GLA_CHUNK
    cc_i = lax.broadcasted_iota(jnp.int32, (GLA_HEADS * GLA_CHUNK, GLA_CHUNK), 1)
    causal = rr_i >= cc_i
    state = state_ref[...]
    for c in range(tq // GLA_CHUNK):
        rows = slice(c * GLA_CHUNK, (c + 1) * GLA_CHUNK)
        qc = qp[rows, :]
        lhs = jnp.concatenate([jnp.where(lane_head == hh, qc, 0.0) for hh in range(GLA_HEADS)], axis=0).astype(BF16)
        a = jnp.where(causal, _dot_nt(lhs, kp[rows, :]), 0.0)
        vc = vf[rows, :]
        o_intra = _dot(a.astype(BF16), vc)
        o_inter = _dot(lhs, state.astype(BF16))
        kv = _dot(kpp[rows, :].T.astype(BF16), vc)
        dcol = dec[rows, :].T[:, 0:1]
        kvd = []
        for hh in range(GLA_HEADS):
            hr = slice(hh * GLA_DK, (hh + 1) * GLA_DK)
            hc = slice(hh * GLA_DV, (hh + 1) * GLA_DV)
            obuf[rows, hc] = o_inter[hr, :] + o_intra[hr, hc]
            kvd.append(kv[hr, hc])
        state = state * dcol + jnp.concatenate(kvd, axis=0)
    state_ref[...] = state

    for hh in range(GLA_HEADS):
        hc = slice(hh * GLA_DV, (hh + 1) * GLA_DV)
        o = _rmsnorm_rows(obuf[:, hc], onorm_ref[...])
        r = rr[:, hc]
        yd_ref[0, :, hc] = (o * (r * jax.nn.sigmoid(r))).astype(BF16)


def _odd_in_proj(x, norm_g, w_in, ln_g, ln_b, sg_w, sg_b, w_gate, b_gate, o_norm):
    bsz, s_len, _ = x.shape
    tq = min(TOK_TILE, s_len)
    hk = GLA_HEADS * GLA_DK
    c_glr = 2 * SG_WIDTH + 2 * hk + GLA_HEADS * GLA_DV
    w = jnp.concatenate([w_in[:, :c_glr], w_in[:, c_glr + GLA_RANK:], w_in[:, c_glr:c_glr + GLA_RANK]], axis=1)
    w = jnp.pad(w, ((0, 0), (0, ODD_PAD - w.shape[1]))).astype(BF16)
    wm = jnp.tril(sg_w)
    sgw = wm.reshape(SG_GROUPS // 2, 2, SG_CHUNK, SG_CHUNK).transpose(0, 2, 1, 3)
    sgw = sgw.reshape(SG_GROUPS // 2, SG_CHUNK, 2 * SG_CHUNK).astype(BF16)
    sgb = jnp.repeat(sg_b.T, SG_GROUP_DIM, axis=1)
    wgate = jnp.pad(w_gate, ((0, LANES - GLA_RANK), (0, 0))).astype(BF16)
    idx = np.arange(tq)
    same = (idx[:, None] // GLA_CHUNK) == (idx[None, :] // GLA_CHUNK)
    lblk = jnp.asarray(same & (idx[None, :] <= idx[:, None]), BF16)
    oblk = jnp.asarray(same, BF16)
    consts = [norm_g.reshape(1, D_MODEL), w, ln_g.reshape(1, -1), ln_b.reshape(1, -1), sgw, sgb, wgate,
              b_gate.reshape(1, -1), o_norm.reshape(1, -1), lblk, oblk]
    tok = lambda width: pl.BlockSpec((1, tq, width), lambda b, j: (b, j, 0))
    return pl.pallas_call(
        _odd_in_kernel,
        grid=(bsz, s_len // tq),
        in_specs=[tok(D_MODEL)] + [_const_spec(c) for c in consts],
        out_specs=[tok(SG_WIDTH), tok(GLA_HEADS * GLA_DV)],
        out_shape=[jax.ShapeDtypeStruct((bsz, s_len, SG_WIDTH), BF16),
                   jax.ShapeDtypeStruct((bsz, s_len, GLA_HEADS * GLA_DV), BF16)],
        scratch_shapes=[pltpu.VMEM((hk, GLA_DV), F32), pltpu.VMEM((tq, GLA_HEADS * GLA_DV), F32)],
        compiler_params=pltpu.CompilerParams(dimension_semantics=("arbitrary", "arbitrary"),
                                             vmem_limit_bytes=VMEM_LIMIT),
        name="odd_in_proj",
    )(x, *consts)


def kernel(x, positions, ev_norm, ev_w_in, sc_conv_w, q_norm, k_norm, ev_w_out, od_norm, od_w_in, sg_ln_g, sg_ln_b,
           sg_w, sg_b, gla_w_gate, gla_b_gate, gla_o_norm, od_w_out, ffn_norm, ffn_w_up, ffn_conv_w, ffn_conv_b,
           ffn_w_down):
    depth = ffn_norm.shape[0]
    cs = _rope_tables(positions)
    for layer in range(depth):
        i = layer // 2
        if layer % 2 == 0:
            ya, q, k, v, iq, ik, iw = _even_in_proj(x, cs, ev_norm[i], ev_w_in[i], sc_conv_w[i], q_norm[i], k_norm[i])
            yb = _sparse_attention(q, k, v, iq, ik, iw)
            w_out = ev_w_out[i]
        else:
            ya, yb = _odd_in_proj(x, od_norm[i], od_w_in[i], sg_ln_g[i], sg_ln_b[i], sg_w[i], sg_b[i],
                                  gla_w_gate[i], gla_b_gate[i], gla_o_norm[i])
            w_out = od_w_out[i]
        x = _mix_ffn(x, ya, yb, w_out, ffn_norm[layer], ffn_w_up[layer], ffn_conv_w[layer], ffn_conv_b[layer],
                     ffn_w_down[layer])
    return x
```

```python
import functools

import numpy as np
import jax
import jax.numpy as jnp
from jax import lax
from jax.experimental import pallas as pl
from jax.experimental.pallas import tpu as pltpu

F32 = jnp.float32
BF16 = jnp.bfloat16

D_MODEL = 1024
SC_WIDTH = 512
CONV_WIDTH = 3
ATT_HEADS = 8
HEAD_DIM = 64
IDX_HEADS = 4
IDX_DIM = 64
TOPK_MAX = 256
ROT_DIM = HEAD_DIM // 4
ROT_HALF = ROT_DIM // 2
ROPE_THETA = 500000.0
IDX_SCALE = (IDX_HEADS * IDX_DIM) ** -0.5
ATT_SCALE = HEAD_DIM ** -0.5
SG_GROUPS = 8
SG_GROUP_DIM = 64
SG_WIDTH = 512
SG_CHUNK = 128
GLA_HEADS = 4
GLA_DK = 64
GLA_DV = 128
GLA_RANK = 16
GLA_CHUNK = 64
GATE_NORMALIZER = 16.0
D_FF = 2816
EPS = 1e-6
LN_EPS = 1e-5

EVEN_COLS = 2500
EVEN_PAD = 2560
ODD_PAD = 2688

LANES = 128
SUBLANES = 8
VMEM_LIMIT = 56 * 1024 * 1024

TOK_TILE = 512
ATT_TILE = 256
COUNT_STEPS = 12
MAX_TIGHTEN_STEPS = 512
FF_CHUNK = 256
NEG_BIG = -1e30
LOG2E = 1.4426950408889634


def _const_spec(arr):
    nd = arr.ndim
    return pl.BlockSpec(arr.shape, lambda *_: (0,) * nd, pipeline_mode=pl.Buffered(1))


def _dot(a, b):
    return jnp.dot(a, b, preferred_element_type=F32)


def _dot_nt(a, b):
    return lax.dot_general(a, b, (((1,), (1,)), ((), ())), preferred_element_type=F32)


def _split_bf16(a):
    hi = a.astype(BF16)
    lo = (a - hi.astype(F32)).astype(BF16)
    return hi, lo


def _dot_hilo(a, b):
    hi, lo = _split_bf16(a)
    return _dot(hi, b) + _dot(lo, b)


def _dot_hilo_left(b, a):
    hi, lo = _split_bf16(a)
    return _dot(b, hi) + _dot(b, lo)


def _rmsnorm_rows(x, g):
    return x * lax.rsqrt(jnp.mean(x * x, axis=-1, keepdims=True) + EPS) * g


def _rope_kernel(invf_ref, pos_ref, cos_ref, sin_ref):
    pos = pos_ref[...]
    for f in range(ROT_HALF):
        ang = pos * invf_ref[f]
        cos_ref[f] = jnp.cos(ang)
        sin_ref[f] = jnp.sin(ang)


def _rope_tables(positions):
    bsz, s_len = positions.shape
    inv_freq = ROPE_THETA ** (-jnp.arange(0, ROT_DIM, 2, dtype=F32) / ROT_DIM)
    cos, sin = pl.pallas_call(
        _rope_kernel,
        out_shape=[jax.ShapeDtypeStruct((ROT_HALF, bsz, s_len), F32)] * 2,
        in_specs=[pl.BlockSpec(memory_space=pltpu.SMEM), pl.BlockSpec(memory_space=pltpu.VMEM)],
        out_specs=[pl.BlockSpec(memory_space=pltpu.VMEM)] * 2,
        name="rope_tables",
    )(inv_freq, positions.astype(F32))
    return jnp.concatenate([cos, sin], axis=0).transpose(1, 2, 0)


def _rope_expand_consts():
    expand = np.zeros((2 * ROT_HALF, 4 * LANES), np.float32)
    add = np.zeros((1, 4 * LANES), np.float32)
    for l in range(LANES):
        m = l % HEAD_DIM
        if m < ROT_HALF:
            expand[m, l] = 1.0
            expand[ROT_HALF + m, LANES + l] = -1.0
        elif m < ROT_DIM:
            expand[m - ROT_HALF, l] = 1.0
            expand[m, LANES + l] = 1.0
        else:
            add[0, l] = 1.0
        if l < HEAD_DIM:
            expand[:, 2 * LANES + l] = expand[:, l]
            expand[:, 3 * LANES + l] = expand[:, LANES + l]
            add[0, 2 * LANES + l] = add[0, l]
        else:
            add[0, 2 * LANES + l] = 1.0
    return expand, add


def _rope_apply(x, cos, sin_signed, first_half):
    width = x.shape[1]
    partner = jnp.where(first_half, pltpu.roll(x, width - ROT_HALF, 1), pltpu.roll(x, ROT_HALF, 1))
    return x * cos + partner * sin_signed


def _even_in_kernel(x_ref, g_ref, w_ref, cw_ref, qg_ref, kg_ref, cs_ref, ex_ref, exadd_ref, bdq_ref, bdk_ref,
                    ya_ref, q_ref, k_ref, v_ref, iq_ref, ik_ref, iw_ref, zbuf):
    tq = x_ref.shape[1]

    @pl.when(pl.program_id(1) == 0)
    def _():
        zbuf[0:SUBLANES, :] = jnp.zeros((SUBLANES, SC_WIDTH), F32)

    h = _rmsnorm_rows(x_ref[0], g_ref[...]).astype(BF16)

    p_sc = _dot(h, w_ref[:, 0:3 * SC_WIDTH])
    bg = p_sc[:, 0:SC_WIDTH]
    z = p_sc[:, SC_WIDTH:2 * SC_WIDTH] * p_sc[:, 2 * SC_WIDTH:3 * SC_WIDTH]
    zbuf[SUBLANES:SUBLANES + tq, :] = z
    cw = cw_ref[...]
    conv = (cw[2:3, :] * z + cw[1:2, :] * zbuf[SUBLANES - 1:SUBLANES - 1 + tq, :]
            + cw[0:1, :] * zbuf[SUBLANES - 2:SUBLANES - 2 + tq, :])
    ya_ref[0] = (bg * conv).astype(BF16)
    zbuf[0:SUBLANES, :] = zbuf[tq:tq + SUBLANES, :]

    tab = _dot_hilo(cs_ref[0], ex_ref[...]) + exadd_ref[...]
    cos_p, sin_p = tab[:, 0:LANES], tab[:, LANES:2 * LANES]
    cos_k, sin_k = tab[:, 2 * LANES:3 * LANES], tab[:, 3 * LANES:4 * LANES]
    lane = lax.broadcasted_iota(jnp.int32, (tq, LANES), 1)
    first_half = (lane % HEAD_DIM) < ROT_HALF

    c0 = 3 * SC_WIDTH
    pq = _dot(h, w_ref[:, c0:c0 + ATT_HEADS * HEAD_DIM])
    ms = _dot_hilo(pq * pq, bdq_ref[...])
    qn = pq * lax.rsqrt(ms + EPS) * qg_ref[...]
    qr = _rope_apply(qn, jnp.concatenate([cos_p] * 4, axis=1), jnp.concatenate([sin_p] * 4, axis=1),
                     jnp.concatenate([first_half] * 4, axis=1)) * (ATT_SCALE * LOG2E)
    qt = qr.T.astype(BF16)
    pad = jnp.zeros((LANES - HEAD_DIM, tq), BF16)
    for hh in range(ATT_HEADS):
        q_ref[0, hh, 0:HEAD_DIM, :] = qt[hh * HEAD_DIM:(hh + 1) * HEAD_DIM, :]
        q_ref[0, hh, HEAD_DIM:LANES, :] = pad

    c0 += ATT_HEADS * HEAD_DIM
    pkv = _dot(h, w_ref[:, c0:c0 + LANES])
    ms_k = _dot_hilo(pkv * pkv, bdk_ref[...])
    is_key = lane < HEAD_DIM
    kvn = pkv * jnp.where(is_key, lax.rsqrt(ms_k + EPS) * kg_ref[...], 1.0)
    kvr = _rope_apply(kvn, cos_k, sin_k, first_half)
    k_ref[0] = jnp.where(is_key, kvr, jnp.where(lane == HEAD_DIM, 1.0, 0.0)).astype(BF16)
    vt = jnp.concatenate([kvr.T[HEAD_DIM:LANES, :], jnp.ones((LANES - HEAD_DIM, tq), F32)], axis=0).astype(BF16)
    for cc in range(tq // ATT_TILE):
        v_ref[0, cc] = vt[:, cc * ATT_TILE:(cc + 1) * ATT_TILE]

    c0 += LANES
    piq = _dot(h, w_ref[:, c0:c0 + IDX_HEADS * IDX_DIM])
    iqr = _rope_apply(piq, jnp.concatenate([cos_p] * 2, axis=1), jnp.concatenate([sin_p] * 2, axis=1),
                      jnp.concatenate([first_half] * 2, axis=1))
    iqt = iqr.T.astype(BF16)
    for hh in range(IDX_HEADS):
        iq_ref[0, hh] = iqt[hh * IDX_DIM:(hh + 1) * IDX_DIM, :]

    c0 += IDX_HEADS * IDX_DIM
    pik = _dot(h, w_ref[:, c0:c0 + LANES])
    ikr = _rope_apply(pik, cos_k, sin_k, first_half)
    ik_ref[0] = ikr[:, 0:IDX_DIM].astype(BF16)
    iw_ref[0] = ikr.T[IDX_DIM:IDX_DIM + IDX_HEADS, :] * IDX_SCALE


def _even_in_proj(x, cs, norm_g, w_in, conv_w, q_g, k_g):
    bsz, s_len, _ = x.shape
    tq = min(TOK_TILE, s_len)
    w = jnp.pad(w_in, ((0, 0), (0, EVEN_PAD - EVEN_COLS))).astype(BF16)
    expand, add = _rope_expand_consts()
    hd = np.arange(ATT_HEADS * HEAD_DIM) // HEAD_DIM
    bdq = jnp.asarray((hd[:, None] == hd[None, :]) / HEAD_DIM, BF16)
    kk = np.arange(LANES)
    bdk = jnp.asarray(((kk[:, None] < HEAD_DIM) & (kk[None, :] < HEAD_DIM)) / HEAD_DIM, BF16)
    consts = [
        norm_g.reshape(1, D_MODEL), w, conv_w,
        jnp.tile(q_g, ATT_HEADS).reshape(1, -1),
        jnp.concatenate([k_g, jnp.ones((LANES - HEAD_DIM,), F32)]).reshape(1, LANES),
    ]
    consts2 = [jnp.asarray(expand, BF16), jnp.asarray(add, F32), bdq, bdk]
    tok = lambda width: pl.BlockSpec((1, tq, width), lambda b, j: (b, j, 0))
    per_tile = tq // ATT_TILE
    heads_t = lambda n, d: pl.BlockSpec((1, n, d, tq), lambda b, j: (b, 0, 0, j))
    return pl.pallas_call(
        _even_in_kernel,
        grid=(bsz, s_len // tq),
        in_specs=[tok(D_MODEL)] + [_const_spec(c) for c in consts] + [tok(2 * ROT_HALF)]
                 + [_const_spec(c) for c in consts2],
        out_specs=[tok(SC_WIDTH), heads_t(ATT_HEADS, LANES), tok(LANES),
                   pl.BlockSpec((1, per_tile, LANES, ATT_TILE), lambda b, j: (b, j, 0, 0)),
                   heads_t(IDX_HEADS, IDX_DIM), tok(IDX_DIM),
                   pl.BlockSpec((1, IDX_HEADS, tq), lambda b, j: (b, 0, j))],
        out_shape=[
            jax.ShapeDtypeStruct((bsz, s_len, SC_WIDTH), BF16),
            jax.ShapeDtypeStruct((bsz, ATT_HEADS, LANES, s_len), BF16),
            jax.ShapeDtypeStruct((bsz, s_len, LANES), BF16),
            jax.ShapeDtypeStruct((bsz, s_len // ATT_TILE, LANES, ATT_TILE), BF16),
            jax.ShapeDtypeStruct((bsz, IDX_HEADS, IDX_DIM, s_len), BF16),
            jax.ShapeDtypeStruct((bsz, s_len, IDX_DIM), BF16),
            jax.ShapeDtypeStruct((bsz, IDX_HEADS, s_len), F32),
        ],
        scratch_shapes=[pltpu.VMEM((tq + SUBLANES, SC_WIDTH), F32)],
        compiler_params=pltpu.CompilerParams(dimension_semantics=("arbitrary", "arbitrary"),
                                             vmem_limit_bytes=VMEM_LIMIT),
        name="even_in_proj",
    )(x, *consts, cs, *consts2)


def _group_reduce(x, op, group=SUBLANES):
    parts = [x[r:r + group, :] for r in range(0, x.shape[0], group)]
    while len(parts) > 1:
        nxt = [op(parts[a], parts[a + 1]) for a in range(0, len(parts) - 1, 2)]
        if len(parts) % 2:
            nxt.append(parts[-1])
        parts = nxt
    return parts[0]


def _rep8(x8, op):
    return jnp.broadcast_to(op(x8, axis=0, keepdims=True), x8.shape)


def _dsa_kernel(iq_ref, iw_ref, q_ref, ik_ref, k_ref, v_ref, tri_ref, ones_ref, out_ref,
                sc_ref, bias_ref, iqa_ref, qa_ref, acc_ref, *, topk):
    tq = ATT_TILE
    ck = ATT_TILE
    i = pl.program_id(1)
    nk = i + 1
    qpos = i * tq + lax.broadcasted_iota(jnp.int32, (SUBLANES, tq), 1)
    keff = jnp.minimum(qpos + 1, topk).astype(F32)
    iw = iw_ref[0]
    inf8 = jnp.full((SUBLANES, tq), jnp.inf, F32)

    for hh in range(IDX_HEADS):
        iqa_ref[:, hh * tq:(hh + 1) * tq] = iq_ref[0, hh]

    def scores(c):
        off = pl.multiple_of(c * ck, ck)
        z = _dot(ik_ref[0, pl.ds(off, ck), :], iqa_ref[...])
        acc = jnp.zeros((ck, tq), F32)
        for hh in range(IDX_HEADS):
            acc = acc + jnp.maximum(z[:, hh * tq:(hh + 1) * tq], 0.0) * iw[hh:hh + 1, :]
        return acc

    def score_body(c, carry):
        lo, hi = carry
        s = scores(c)
        sc_ref[c] = s
        return jnp.minimum(lo, _group_reduce(s, jnp.minimum)), jnp.maximum(hi, _group_reduce(s, jnp.maximum))

    lo, hi = lax.fori_loop(0, i, score_body, (inf8, -inf8))
    s = scores(i)
    kpos = i * ck + lax.broadcasted_iota(jnp.int32, (ck, tq), 0)
    valid = kpos <= i * tq + lax.broadcasted_iota(jnp.int32, (ck, tq), 1)
    sc_ref[i] = jnp.where(valid, s, -jnp.inf)
    lo = _rep8(jnp.minimum(lo, _group_reduce(jnp.where(valid, s, jnp.inf), jnp.minimum)), jnp.min)
    hi = _rep8(jnp.maximum(hi, _group_reduce(jnp.where(valid, s, -jnp.inf), jnp.maximum)), jnp.max)

    def pivot(lo, hi):
        mid = 0.5 * lo + 0.5 * hi
        return jnp.where(mid < hi, jnp.maximum(mid, lo), lo)

    def count_body(_, carry):
        lo, hi, chi = carry
        v = pivot(lo, hi)

        def body(c, accs):
            accs = list(accs)
            for r in range(ck // SUBLANES):
                x = sc_ref[c, r * SUBLANES:(r + 1) * SUBLANES, :]
                accs[r % 4] = accs[r % 4] + jnp.where(x > v, 1.0, 0.0)
            return tuple(accs)

        zero = jnp.zeros((SUBLANES, tq), F32)
        a0, a1, a2, a3 = lax.fori_loop(0, nk, body, (zero, zero, zero, zero))
        cnt = _rep8((a0 + a1) + (a2 + a3), jnp.sum)
        ge = cnt >= keff
        return jnp.where(ge, v, lo), jnp.where(ge, hi, v), jnp.where(ge, chi, cnt)

    lo, hi, chi = lax.fori_loop(0, COUNT_STEPS, count_body, (lo, hi, jnp.zeros((SUBLANES, tq), F32)))

    def tight_cond(state):
        it, pending = state[0], state[1]
        return jnp.logical_and(pending > 0, it < MAX_TIGHTEN_STEPS)

    def tight_body(state):
        it, _, lo, hi, chi = state
        v = pivot(lo, hi)

        def body(c, accs):
            cnt, below, above = accs
            for r in range(ck // SUBLANES):
                x = sc_ref[c, r * SUBLANES:(r + 1) * SUBLANES, :]
                gt = x > v
                cnt = cnt + jnp.where(gt, 1.0, 0.0)
                below = jnp.maximum(below, jnp.where(gt, -jnp.inf, x))
                above = jnp.minimum(above, jnp.where(gt, x, jnp.inf))
            return cnt, below, above

        cnt, below, above = lax.fori_loop(0, nk, body, (jnp.zeros((SUBLANES, tq), F32), -inf8, inf8))
        cnt = _rep8(cnt, jnp.sum)
        below = _rep8(below, jnp.max)
        above = _rep8(above, jnp.min)
        ge = cnt >= keff
        lo = jnp.where(ge, above, lo)
        chi = jnp.where(ge, chi, cnt)
        hi = jnp.where(ge, hi, below)
        pending = jnp.max(jnp.where(lo < hi, 1.0, 0.0)) > 0.5
        return it + 1, pending.astype(jnp.int32), lo, hi, chi

    state = lax.while_loop(tight_cond, tight_body, (jnp.int32(0), jnp.int32(1), lo, hi, chi))
    thr = state[3][0:1, :]
    ties_wanted = (keff - state[4])[0:1, :]

    def mask_body(c, tie_carry):
        x = sc_ref[c]
        eq = x == thr
        eqb = jnp.where(eq, 1.0, 0.0).astype(BF16)
        prefix = _dot(tri_ref[...], eqb)
        sel = jnp.logical_or(x > thr, jnp.logical_and(eq, prefix <= ties_wanted - tie_carry))
        bias_ref[c] = jnp.where(sel, 0.0, NEG_BIG).astype(BF16)
        return tie_carry + _dot(ones_ref[...], eqb)[0:1, :]

    lax.fori_loop(0, nk, mask_body, jnp.zeros((1, tq), F32))

    for hh in range(ATT_HEADS):
        qa_ref[:, hh * tq:(hh + 1) * tq] = q_ref[0, hh]
    acc_ref[...] = jnp.zeros(acc_ref.shape, F32)

    def att_body(c, m_old):
        off = pl.multiple_of(c * ck, ck)
        sb = _dot(k_ref[0, pl.ds(off, ck), :], qa_ref[...]).astype(BF16)
        sb = sb + jnp.concatenate([bias_ref[c]] * ATT_HEADS, axis=1)
        cm = jnp.max(_group_reduce(sb, jnp.maximum, 2 * SUBLANES).astype(F32), axis=0, keepdims=True)
        m_new = jnp.maximum(m_old, cm)
        p = jnp.exp2(sb - m_new.astype(BF16))
        acc_ref[...] = acc_ref[...] * jnp.exp2(m_old - m_new) + _dot(v_ref[0, c], p)
        return m_new

    lax.fori_loop(0, nk, att_body, jnp.full((1, ATT_HEADS * tq), NEG_BIG, BF16).astype(F32))
    outs = []
    for hh in range(ATT_HEADS):
        a = acc_ref[:, hh * tq:(hh + 1) * tq]
        outs.append(a[0:HEAD_DIM, :] * (1.0 / a[HEAD_DIM:HEAD_DIM + 1, :]))
    out_ref[0] = jnp.concatenate(outs, axis=0).T.astype(BF16)


def _sparse_attention(qt, k_aug, vt_ext, iqt, ik, iwt):
    bsz, _, _, s_len = qt.shape
    tq = ATT_TILE
    topk = min(TOPK_MAX, s_len // 4)
    n_chunks = s_len // tq
    idx = np.arange(tq)
    tri = jnp.asarray(idx[None, :] <= idx[:, None], BF16)
    ones = jnp.ones((SUBLANES, tq), BF16)
    heads_t = lambda n, d: pl.BlockSpec((1, n, d, tq), lambda b, j: (b, 0, 0, j))
    return pl.pallas_call(
        functools.partial(_dsa_kernel, topk=topk),
        grid=(bsz, n_chunks),
        in_specs=[heads_t(IDX_HEADS, IDX_DIM), pl.BlockSpec((1, IDX_HEADS, tq), lambda b, j: (b, 0, j)),
                  heads_t(ATT_HEADS, LANES),
                  pl.BlockSpec((1, s_len, IDX_DIM), lambda b, j: (b, 0, 0)),
                  pl.BlockSpec((1, s_len, LANES), lambda b, j: (b, 0, 0)),
                  pl.BlockSpec((1, n_chunks, LANES, tq), lambda b, j: (b, 0, 0, 0)),
                  _const_spec(tri), _const_spec(ones)],
        out_specs=pl.BlockSpec((1, tq, ATT_HEADS * HEAD_DIM), lambda b, j: (b, j, 0)),
        out_shape=jax.ShapeDtypeStruct((bsz, s_len, ATT_HEADS * HEAD_DIM), BF16),
        scratch_shapes=[
            pltpu.VMEM((n_chunks, tq, tq), F32),
            pltpu.VMEM((n_chunks, tq, tq), BF16),
            pltpu.VMEM((IDX_DIM, IDX_HEADS * tq), BF16),
            pltpu.VMEM((LANES, ATT_HEADS * tq), BF16),
            pltpu.VMEM((LANES, ATT_HEADS * tq), F32),
        ],
        compiler_params=pltpu.CompilerParams(dimension_semantics=("arbitrary", "arbitrary"),
                                             vmem_limit_bytes=VMEM_LIMIT),
        name="sparse_attention",
    )(iqt, iwt, qt, ik, k_aug, vt_ext, tri, ones)


def _mix_ffn_kernel(x_ref, ya_ref, yb_ref, wo_ref, fg_ref, wg_ref, wu_ref, wd_ref, cwg_ref, cwu_ref, cbg_ref,
                    cbu_ref, out_ref, h_ref, acc_ref, gbuf, ubuf, carry_ref):
    tq = x_ref.shape[1]
    half = ya_ref.shape[2]
    n_ff = wg_ref.shape[0]

    @pl.when(pl.program_id(1) == 0)
    def _():
        carry_ref[...] = jnp.zeros(carry_ref.shape, F32)

    x1 = x_ref[0] + _dot(ya_ref[0], wo_ref[0:half, :]) + _dot(yb_ref[0], wo_ref[half:2 * half, :])
    acc_ref[...] = x1
    h_ref[...] = _rmsnorm_rows(x1, fg_ref[...]).astype(BF16)

    def conv(u, buf, c, slot, cw, cb):
        buf[0:SUBLANES, :] = carry_ref[c, slot]
        buf[SUBLANES:SUBLANES + tq, :] = u
        y = (cw[2:3, :] * u + cw[1:2, :] * buf[SUBLANES - 1:SUBLANES - 1 + tq, :]
             + cw[0:1, :] * buf[SUBLANES - 2:SUBLANES - 2 + tq, :] + cb)
        carry_ref[c, slot] = buf[tq:tq + SUBLANES, :]
        return y

    def body(c, carry):
        h = h_ref[...]
        gate = conv(_dot(h, wg_ref[c]), gbuf, c, 0, cwg_ref[c], cbg_ref[c])
        up = conv(_dot(h, wu_ref[c]), ubuf, c, 1, cwu_ref[c], cbu_ref[c])
        act = gate * jax.nn.sigmoid(gate) * up
        acc_ref[...] += _dot(act.astype(BF16), wd_ref[c])
        return carry

    lax.fori_loop(0, n_ff, body, 0)
    out_ref[0] = acc_ref[...]


def _mix_ffn(x, ya, yb, w_out, ffn_g, w_up, conv_w, conv_b, w_down):
    bsz, s_len, _ = x.shape
    tq = min(TOK_TILE, s_len)
    half = ya.shape[2]
    n_ff = D_FF // FF_CHUNK
    chunked = lambda a: a.reshape(a.shape[0], n_ff, FF_CHUNK).transpose(1, 0, 2)
    consts = [
        w_out.astype(BF16), ffn_g.reshape(1, D_MODEL),
        chunked(w_up[:, :D_FF]).astype(BF16), chunked(w_up[:, D_FF:]).astype(BF16),
        w_down.reshape(n_ff, FF_CHUNK, D_MODEL).astype(BF16),
        chunked(conv_w[:, :D_FF]), chunked(conv_w[:, D_FF:]),
        chunked(conv_b[None, :D_FF]), chunked(conv_b[None, D_FF:]),
    ]
    tok = lambda width: pl.BlockSpec((1, tq, width), lambda b, j: (b, j, 0))
    return pl.pallas_call(
        _mix_ffn_kernel,
        grid=(bsz, s_len // tq),
        in_specs=[tok(D_MODEL), tok(half), tok(half)] + [_const_spec(c) for c in consts],
        out_specs=tok(D_MODEL),
        out_shape=jax.ShapeDtypeStruct((bsz, s_len, D_MODEL), F32),
        scratch_shapes=[
            pltpu.VMEM((tq, D_MODEL), BF16),
            pltpu.VMEM((tq, D_MODEL), F32),
            pltpu.VMEM((tq + SUBLANES, FF_CHUNK), F32),
            pltpu.VMEM((tq + SUBLANES, FF_CHUNK), F32),
            pltpu.VMEM((n_ff, 2, SUBLANES, FF_CHUNK), F32),
        ],
        compiler_params=pltpu.CompilerParams(dimension_semantics=("arbitrary", "arbitrary"),
                                             vmem_limit_bytes=VMEM_LIMIT),
        name="mix_out_ffn",
    )(x, ya, yb, *consts)


def _odd_in_kernel(x_ref, g_ref, w_ref, lng_ref, lnb_ref, sgw_ref, sgb_ref, wgate_ref, bgate_ref, onorm_ref,
                   lblk_ref, oblk_ref, yc_ref, yd_ref, state_ref, obuf):
    tq = x_ref.shape[1]
    hk = GLA_HEADS * GLA_DK

    @pl.when(pl.program_id(1) == 0)
    def _():
        state_ref[...] = jnp.zeros(state_ref.shape, F32)

    h = _rmsnorm_rows(x_ref[0], g_ref[...]).astype(BF16)

    uv = _dot(h, w_ref[:, 0:2 * SG_WIDTH])
    gel = 0.5 * uv * (1.0 + lax.erf(uv * np.float32(np.sqrt(0.5))))
    u = gel[:, 0:SG_WIDTH]
    vv = gel[:, SG_WIDTH:2 * SG_WIDTH]
    mu = jnp.mean(vv, axis=-1, keepdims=True)
    xc = vv - mu
    vn = (xc * lax.rsqrt(jnp.mean(xc * xc, axis=-1, keepdims=True) + LN_EPS) * lng_ref[...] + lnb_ref[...])
    vn = vn.astype(BF16)
    low = lax.broadcasted_iota(jnp.int32, (SG_CHUNK, LANES), 1) < SG_GROUP_DIM
    zero = jnp.zeros((SG_CHUNK, LANES), BF16)
    for cc in range(tq // SG_CHUNK):
        rows = slice(cc * SG_CHUNK, (cc + 1) * SG_CHUNK)
        for p in range(SG_WIDTH // LANES):
            cols = slice(p * LANES, (p + 1) * LANES)
            tile = vn[rows, cols]
            rhs = jnp.concatenate([jnp.where(low, tile, zero), jnp.where(low, zero, tile)], axis=0)
            sp = _dot(sgw_ref[p], rhs) + sgb_ref[:, cols]
            yc_ref[0, rows, cols] = (u[rows, cols] * sp).astype(BF16)

    c0 = 2 * SG_WIDTH
    qf = _dot(h, w_ref[:, c0:c0 + hk]) * (GLA_DK ** -0.5)
    kf = _dot(h, w_ref[:, c0 + hk:c0 + 2 * hk])
    c0 += 2 * hk
    vf = _dot(h, w_ref[:, c0:c0 + GLA_HEADS * GLA_DV]).astype(BF16)
    c0 += GLA_HEADS * GLA_DV
    rr = _dot(h, w_ref[:, c0:c0 + GLA_HEADS * GLA_DV])
    c0 += GLA_HEADS * GLA_DV
    pg = _dot(h, w_ref[:, c0:c0 + LANES])
    g = jax.nn.log_sigmoid(_dot(pg.astype(BF16), wgate_ref[...]) + bgate_ref[...]) * (1.0 / GATE_NORMALIZER)
    b = _dot_hilo_left(lblk_ref[...], g)
    gtot = _dot_hilo_left(oblk_ref[...], g)
    qp = qf * jnp.exp(b)
    kp = (kf * jnp.exp(-b)).astype(BF16)
    kpp = kf * jnp.exp(gtot - b)
    dec = jnp.exp(gtot)

    lane_head = lax.broadcasted_iota(jnp.int32, (GLA_CHUNK, hk), 1) // GLA_DK
    rr_i = lax.broadcasted_iota(jnp.int32, (GLA_HEADS * GLA_CHUNK, GLA_CHUNK), 0) % GLA_CHUNK
    cc_i = lax.broadcasted_iota(jnp.int32, (GLA_HEADS * GLA_CHUNK, GLA_CHUNK), 1)
    causal = rr_i >= cc_i
    state = state_ref[...]
    for c in range(tq // GLA_CHUNK):
        rows = slice(c * GLA_CHUNK, (c + 1) * GLA_CHUNK)
        qc = qp[rows, :]
        lhs = jnp.concatenate([jnp.where(lane_head == hh, qc, 0.0) for hh in range(GLA_HEADS)], axis=0).astype(BF16)
        a = jnp.where(causal, _dot_nt(lhs, kp[rows, :]), 0.0)
        vc = vf[rows, :]
        o_intra = _dot(a.astype(BF16), vc)
        o_inter = _dot(lhs, state.astype(BF16))
        kv = _dot(kpp[rows, :].T.astype(BF16), vc)
        dcol = dec[rows, :].T[:, 0:1]
        kvd = []
        for hh in range(GLA_HEADS):
            hr = slice(hh * GLA_DK, (hh + 1) * GLA_DK)
            hc = slice(hh * GLA_DV, (hh + 1) * GLA_DV)
            obuf[rows, hc] = o_inter[hr, :] + o_intra[hr, hc]
            kvd.append(kv[hr, hc])
        state = state * dcol + jnp.concatenate(kvd, axis=0)
    state_ref[...] = state

    for hh in range(GLA_HEADS):
        hc = slice(hh * GLA_DV, (hh + 1) * GLA_DV)
        o = _rmsnorm_rows(obuf[:, hc], onorm_ref[...])
        r = rr[:, hc]
        yd_ref[0, :, hc] = (o * (r * jax.nn.sigmoid(r))).astype(BF16)


def _odd_in_proj(x, norm_g, w_in, ln_g, ln_b, sg_w, sg_b, w_gate, b_gate, o_norm):
    bsz, s_len, _ = x.shape
    tq = min(TOK_TILE, s_len)
    hk = GLA_HEADS * GLA_DK
    c_glr = 2 * SG_WIDTH + 2 * hk + GLA_HEADS * GLA_DV
    w = jnp.concatenate([w_in[:, :c_glr], w_in[:, c_glr + GLA_RANK:], w_in[:, c_glr:c_glr + GLA_RANK]], axis=1)
    w = jnp.pad(w, ((0, 0), (0, ODD_PAD - w.shape[1]))).astype(BF16)
    wm = jnp.tril(sg_w)
    sgw = wm.reshape(SG_GROUPS // 2, 2, SG_CHUNK, SG_CHUNK).transpose(0, 2, 1, 3)
    sgw = sgw.reshape(SG_GROUPS // 2, SG_CHUNK, 2 * SG_CHUNK).astype(BF16)
    sgb = jnp.repeat(sg_b.T, SG_GROUP_DIM, axis=1)
    wgate = jnp.pad(w_gate, ((0, LANES - GLA_RANK), (0, 0))).astype(BF16)
    idx = np.arange(tq)
    same = (idx[:, None] // GLA_CHUNK) == (idx[None, :] // GLA_CHUNK)
    lblk = jnp.asarray(same & (idx[None, :] <= idx[:, None]), BF16)
    oblk = jnp.asarray(same, BF16)
    consts = [norm_g.reshape(1, D_MODEL), w, ln_g.reshape(1, -1), ln_b.reshape(1, -1), sgw, sgb, wgate,
              b_gate.reshape(1, -1), o_norm.reshape(1, -1), lblk, oblk]
    tok = lambda width: pl.BlockSpec((1, tq, width), lambda b, j: (b, j, 0))
    return pl.pallas_call(
        _odd_in_kernel,
        grid=(bsz, s_len // tq),
        in_specs=[tok(D_MODEL)] + [_const_spec(c) for c in consts],
        out_specs=[tok(SG_WIDTH), tok(GLA_HEADS * GLA_DV)],
        out_shape=[jax.ShapeDtypeStruct((bsz, s_len, SG_WIDTH), BF16),
                   jax.ShapeDtypeStruct((bsz, s_len, GLA_HEADS * GLA_DV), BF16)],
        scratch_shapes=[pltpu.VMEM((hk, GLA_DV), F32), pltpu.VMEM((tq, GLA_HEADS * GLA_DV), F32)],
        compiler_params=pltpu.CompilerParams(dimension_semantics=("arbitrary", "arbitrary"),
                                             vmem_limit_bytes=VMEM_LIMIT),
        name="odd_in_proj",
    )(x, *consts)


def kernel(x, positions, ev_norm, ev_w_in, sc_conv_w, q_norm, k_norm, ev_w_out, od_norm, od_w_in, sg_ln_g, sg_ln_b,
           sg_w, sg_b, gla_w_gate, gla_b_gate, gla_o_norm, od_w_out, ffn_norm, ffn_w_up, ffn_conv_w, ffn_conv_b,
           ffn_w_down):
    depth = ffn_norm.shape[0]
    cs = _rope_tables(positions)
    for layer in range(depth):
        i = layer // 2
        if layer % 2 == 0:
            ya, q, k, v, iq, ik, iw = _even_in_proj(x, cs, ev_norm[i], ev_w_in[i], sc_conv_w[i], q_norm[i], k_norm[i])
            yb = _sparse_attention(q, k, v, iq, ik, iw)
            w_out = ev_w_out[i]
        else:
            ya, yb = _odd_in_proj(x, od_norm[i], od_w_in[i], sg_ln_g[i], sg_ln_b[i], sg_w[i], sg_b[i],
                                  gla_w_gate[i], gla_b_gate[i], gla_o_norm[i])
            w_out = od_w_out[i]
        x = _mix_ffn(x, ya, yb, w_out, ffn_norm[layer], ffn_w_up[layer], ffn_conv_w[layer], ffn_conv_b[layer],
                     ffn_w_down[layer])
    return x
```

```python
import functools

import numpy as np
import jax
import jax.numpy as jnp
from jax import lax
from jax.experimental import pallas as pl
from jax.experimental.pallas import tpu as pltpu

F32 = jnp.float32
BF16 = jnp.bfloat16

D_MODEL = 1024
SC_WIDTH = 512
CONV_WIDTH = 3
ATT_HEADS = 8
HEAD_DIM = 64
IDX_HEADS = 4
IDX_DIM = 64
TOPK_MAX = 256
ROT_DIM = HEAD_DIM // 4
ROT_HALF = ROT_DIM // 2
ROPE_THETA = 500000.0
IDX_SCALE = (IDX_HEADS * IDX_DIM) ** -0.5
ATT_SCALE = HEAD_DIM ** -0.5
SG_GROUPS = 8
SG_GROUP_DIM = 64
SG_WIDTH = 512
SG_CHUNK = 128
GLA_HEADS = 4
GLA_DK = 64
GLA_DV = 128
GLA_RANK = 16
GLA_CHUNK = 64
GATE_NORMALIZER = 16.0
D_FF = 2816
EPS = 1e-6
LN_EPS = 1e-5

EVEN_COLS = 2500
EVEN_PAD = 2560
ODD_PAD = 2688

LANES = 128
SUBLANES = 8
VMEM_LIMIT = 56 * 1024 * 1024

TOK_TILE = 512
ATT_TILE = 256
COUNT_STEPS = 12
MAX_TIGHTEN_STEPS = 512
FF_CHUNK = 256
NEG_BIG = -1e30
LOG2E = 1.4426950408889634


def _const_spec(arr):
    nd = arr.ndim
    return pl.BlockSpec(arr.shape, lambda *_: (0,) * nd, pipeline_mode=pl.Buffered(1))


def _dot(a, b):
    return jnp.dot(a, b, preferred_element_type=F32)


def _dot_nt(a, b):
    return lax.dot_general(a, b, (((1,), (1,)), ((), ())), preferred_element_type=F32)


def _split_bf16(a):
    hi = a.astype(BF16)
    lo = (a - hi.astype(F32)).astype(BF16)
    return hi, lo


def _dot_hilo(a, b):
    hi, lo = _split_bf16(a)
    return _dot(hi, b) + _dot(lo, b)


def _dot_hilo_left(b, a):
    hi, lo = _split_bf16(a)
    return _dot(b, hi) + _dot(b, lo)


def _rmsnorm_rows(x, g):
    return x * lax.rsqrt(jnp.mean(x * x, axis=-1, keepdims=True) + EPS) * g


def _rope_kernel(invf_ref, pos_ref, cos_ref, sin_ref):
    pos = pos_ref[...]
    for f in range(ROT_HALF):
        ang = pos * invf_ref[f]
        cos_ref[f] = jnp.cos(ang)
        sin_ref[f] = jnp.sin(ang)


def _rope_tables(positions):
    bsz, s_len = positions.shape
    inv_freq = ROPE_THETA ** (-jnp.arange(0, ROT_DIM, 2, dtype=F32) / ROT_DIM)
    cos, sin = pl.pallas_call(
        _rope_kernel,
        out_shape=[jax.ShapeDtypeStruct((ROT_HALF, bsz, s_len), F32)] * 2,
        in_specs=[pl.BlockSpec(memory_space=pltpu.SMEM), pl.BlockSpec(memory_space=pltpu.VMEM)],
        out_specs=[pl.BlockSpec(memory_space=pltpu.VMEM)] * 2,
        name="rope_tables",
    )(inv_freq, positions.astype(F32))
    return jnp.concatenate([cos, sin], axis=0).transpose(1, 2, 0)


def _rope_expand_consts():
    expand = np.zeros((2 * ROT_HALF, 4 * LANES), np.float32)
    add = np.zeros((1, 4 * LANES), np.float32)
    for l in range(LANES):
        m = l % HEAD_DIM
        if m < ROT_HALF:
            expand[m, l] = 1.0
            expand[ROT_HALF + m, LANES + l] = -1.0
        elif m < ROT_DIM:
            expand[m - ROT_HALF, l] = 1.0
            expand[m, LANES + l] = 1.0
        else:
            add[0, l] = 1.0
        if l < HEAD_DIM:
            expand[:, 2 * LANES + l] = expand[:, l]
            expand[:, 3 * LANES + l] = expand[:, LANES + l]
            add[0, 2 * LANES + l] = add[0, l]
        else:
            add[0, 2 * LANES + l] = 1.0
    return expand, add


def _rope_apply(x, cos, sin_signed, first_half):
    width = x.shape[1]
    partner = jnp.where(first_half, pltpu.roll(x, width - ROT_HALF, 1), pltpu.roll(x, ROT_HALF, 1))
    return x * cos + partner * sin_signed


def _even_in_kernel(x_ref, g_ref, w_ref, cw_ref, qg_ref, kg_ref, cs_ref, ex_ref, exadd_ref, bdq_ref, bdk_ref,
                    ya_ref, q_ref, k_ref, v_ref, iq_ref, ik_ref, iw_ref, zbuf):
    tq = x_ref.shape[1]

    @pl.when(pl.program_id(1) == 0)
    def _():
        zbuf[0:SUBLANES, :] = jnp.zeros((SUBLANES, SC_WIDTH), F32)

    h = _rmsnorm_rows(x_ref[0], g_ref[...]).astype(BF16)

    p_sc = _dot(h, w_ref[:, 0:3 * SC_WIDTH])
    bg = p_sc[:, 0:SC_WIDTH]
    z = p_sc[:, SC_WIDTH:2 * SC_WIDTH] * p_sc[:, 2 * SC_WIDTH:3 * SC_WIDTH]
    zbuf[SUBLANES:SUBLANES + tq, :] = z
    cw = cw_ref[...]
    conv = (cw[2:3, :] * z + cw[1:2, :] * zbuf[SUBLANES - 1:SUBLANES - 1 + tq, :]
            + cw[0:1, :] * zbuf[SUBLANES - 2:SUBLANES - 2 + tq, :])
    ya_ref[0] = (bg * conv).astype(BF16)
    zbuf[0:SUBLANES, :] = zbuf[tq:tq + SUBLANES, :]

    tab = _dot_hilo(cs_ref[0], ex_ref[...]) + exadd_ref[...]
    cos_p, sin_p = tab[:, 0:LANES], tab[:, LANES:2 * LANES]
    cos_k, sin_k = tab[:, 2 * LANES:3 * LANES], tab[:, 3 * LANES:4 * LANES]
    lane = lax.broadcasted_iota(jnp.int32, (tq, LANES), 1)
    first_half = (lane % HEAD_DIM) < ROT_HALF

    c0 = 3 * SC_WIDTH
    pq = _dot(h, w_ref[:, c0:c0 + ATT_HEADS * HEAD_DIM])
    ms = _dot_hilo(pq * pq, bdq_ref[...])
    qn = pq * lax.rsqrt(ms + EPS) * qg_ref[...]
    qr = _rope_apply(qn, jnp.concatenate([cos_p] * 4, axis=1), jnp.concatenate([sin_p] * 4, axis=1),
                     jnp.concatenate([first_half] * 4, axis=1)) * (ATT_SCALE * LOG2E)
    qt = qr.T.astype(BF16)
    pad = jnp.zeros((LANES - HEAD_DIM, tq), BF16)
    for hh in range(ATT_HEADS):
        q_ref[0, hh, 0:HEAD_DIM, :] = qt[hh * HEAD_DIM:(hh + 1) * HEAD_DIM, :]
        q_ref[0, hh, HEAD_DIM:LANES, :] = pad

    c0 += ATT_HEADS * HEAD_DIM
    pkv = _dot(h, w_ref[:, c0:c0 + LANES])
    ms_k = _dot_hilo(pkv * pkv, bdk_ref[...])
    is_key = lane < HEAD_DIM
    kvn = pkv * jnp.where(is_key, lax.rsqrt(ms_k + EPS) * kg_ref[...], 1.0)
    kvr = _rope_apply(kvn, cos_k, sin_k, first_half)
    k_ref[0] = jnp.where(is_key, kvr, jnp.where(lane == HEAD_DIM, 1.0, 0.0)).astype(BF16)
    vt = jnp.concatenate([kvr.T[HEAD_DIM:LANES, :], jnp.ones((LANES - HEAD_DIM, tq), F32)], axis=0).astype(BF16)
    for cc in range(tq // ATT_TILE):
        v_ref[0, cc] = vt[:, cc * ATT_TILE:(cc + 1) * ATT_TILE]

    c0 += LANES
    piq = _dot(h, w_ref[:, c0:c0 + IDX_HEADS * IDX_DIM])
    iqr = _rope_apply(piq, jnp.concatenate([cos_p] * 2, axis=1), jnp.concatenate([sin_p] * 2, axis=1),
                      jnp.concatenate([first_half] * 2, axis=1))
    iqt = iqr.T.astype(BF16)
    for hh in range(IDX_HEADS):
        iq_ref[0, hh] = iqt[hh * IDX_DIM:(hh + 1) * IDX_DIM, :]

    c0 += IDX_HEADS * IDX_DIM
    pik = _dot(h, w_ref[:, c0:c0 + LANES])
    ikr = _rope_apply(pik, cos_k, sin_k, first_half)
    ik_ref[0] = ikr[:, 0:IDX_DIM].astype(BF16)
    iw_ref[0] = ikr.T[IDX_DIM:IDX_DIM + IDX_HEADS, :] * IDX_SCALE


def _even_in_proj(x, cs, norm_g, w_in, conv_w, q_g, k_g):
    bsz, s_len, _ = x.shape
    tq = min(TOK_TILE, s_len)
    w = jnp.pad(w_in, ((0, 0), (0, EVEN_PAD - EVEN_COLS))).astype(BF16)
    expand, add = _rope_expand_consts()
    hd = np.arange(ATT_HEADS * HEAD_DIM) // HEAD_DIM
    bdq = jnp.asarray((hd[:, None] == hd[None, :]) / HEAD_DIM, BF16)
    kk = np.arange(LANES)
    bdk = jnp.asarray(((kk[:, None] < HEAD_DIM) & (kk[None, :] < HEAD_DIM)) / HEAD_DIM, BF16)
    consts = [
        norm_g.reshape(1, D_MODEL), w, conv_w,
        jnp.tile(q_g, ATT_HEADS).reshape(1, -1),
        jnp.concatenate([k_g, jnp.ones((LANES - HEAD_DIM,), F32)]).reshape(1, LANES),
    ]
    consts2 = [jnp.asarray(expand, BF16), jnp.asarray(add, F32), bdq, bdk]
    tok = lambda width: pl.BlockSpec((1, tq, width), lambda b, j: (b, j, 0))
    per_tile = tq // ATT_TILE
    heads_t = lambda n, d: pl.BlockSpec((1, n, d, tq), lambda b, j: (b, 0, 0, j))
    return pl.pallas_call(
        _even_in_kernel,
        grid=(bsz, s_len // tq),
        in_specs=[tok(D_MODEL)] + [_const_spec(c) for c in consts] + [tok(2 * ROT_HALF)]
                 + [_const_spec(c) for c in consts2],
        out_specs=[tok(SC_WIDTH), heads_t(ATT_HEADS, LANES), tok(LANES),
                   pl.BlockSpec((1, per_tile, LANES, ATT_TILE), lambda b, j: (b, j, 0, 0)),
                   heads_t(IDX_HEADS, IDX_DIM), tok(IDX_DIM),
                   pl.BlockSpec((1, IDX_HEADS, tq), lambda b, j: (b, 0, j))],
        out_shape=[
            jax.ShapeDtypeStruct((bsz, s_len, SC_WIDTH), BF16),
            jax.ShapeDtypeStruct((bsz, ATT_HEADS, LANES, s_len), BF16),
            jax.ShapeDtypeStruct((bsz, s_len, LANES), BF16),
            jax.ShapeDtypeStruct((bsz, s_len // ATT_TILE, LANES, ATT_TILE), BF16),
            jax.ShapeDtypeStruct((bsz, IDX_HEADS, IDX_DIM, s_len), BF16),
            jax.ShapeDtypeStruct((bsz, s_len, IDX_DIM), BF16),
            jax.ShapeDtypeStruct((bsz, IDX_HEADS, s_len), F32),
        ],
        scratch_shapes=[pltpu.VMEM((tq + SUBLANES, SC_WIDTH), F32)],
        compiler_params=pltpu.CompilerParams(dimension_semantics=("arbitrary", "arbitrary"),
                                             vmem_limit_bytes=VMEM_LIMIT),
        name="even_in_proj",
    )(x, *consts, cs, *consts2)


def _group_reduce(x, op, group=SUBLANES):
    parts = [x[r:r + group, :] for r in range(0, x.shape[0], group)]
    while len(parts) > 1:
        nxt = [op(parts[a], parts[a + 1]) for a in range(0, len(parts) - 1, 2)]
        if len(parts) % 2:
            nxt.append(parts[-1])
        parts = nxt
    return parts[0]


def _rep8(x8, op):
    return jnp.broadcast_to(op(x8, axis=0, keepdims=True), x8.shape)


def _dsa_kernel(iq_ref, iw_ref, q_ref, ik_ref, k_ref, v_ref, tri_ref, ones_ref, out_ref,
                sc_ref, bias_ref, iqa_ref, qa_ref, acc_ref, lg_a, lg_b, *, topk):
    tq = ATT_TILE
    ck = ATT_TILE
    i = pl.program_id(1)
    nk = i + 1
    qpos = i * tq + lax.broadcasted_iota(jnp.int32, (SUBLANES, tq), 1)
    keff = jnp.minimum(qpos + 1, topk).astype(F32)
    iw = iw_ref[0]
    inf8 = jnp.full((SUBLANES, tq), jnp.inf, F32)

    for hh in range(IDX_HEADS):
        iqa_ref[:, hh * tq:(hh + 1) * tq] = iq_ref[0, hh]

    def scores(c):
        off = pl.multiple_of(c * ck, ck)
        z = _dot(ik_ref[0, pl.ds(off, ck), :], iqa_ref[...])
        acc = jnp.zeros((ck, tq), F32)
        for hh in range(IDX_HEADS):
            acc = acc + jnp.maximum(z[:, hh * tq:(hh + 1) * tq], 0.0) * iw[hh:hh + 1, :]
        return acc

    def score_body(c, carry):
        lo, hi = carry
        s = scores(c)
        sc_ref[c] = s
        return jnp.minimum(lo, _group_reduce(s, jnp.minimum)), jnp.maximum(hi, _group_reduce(s, jnp.maximum))

    lo, hi = lax.fori_loop(0, i, score_body, (inf8, -inf8))
    s = scores(i)
    kpos = i * ck + lax.broadcasted_iota(jnp.int32, (ck, tq), 0)
    valid = kpos <= i * tq + lax.broadcasted_iota(jnp.int32, (ck, tq), 1)
    sc_ref[i] = jnp.where(valid, s, -jnp.inf)
    lo = _rep8(jnp.minimum(lo, _group_reduce(jnp.where(valid, s, jnp.inf), jnp.minimum)), jnp.min)
    hi = _rep8(jnp.maximum(hi, _group_reduce(jnp.where(valid, s, -jnp.inf), jnp.maximum)), jnp.max)

    def pivot(lo, hi):
        mid = 0.5 * lo + 0.5 * hi
        return jnp.where(mid < hi, jnp.maximum(mid, lo), lo)

    def count_body(_, carry):
        lo, hi, chi = carry
        v = pivot(lo, hi)

        def body(c, accs):
            accs = list(accs)
            for r in range(ck // SUBLANES):
                x = sc_ref[c, r * SUBLANES:(r + 1) * SUBLANES, :]
                accs[r % 4] = accs[r % 4] + jnp.where(x > v, 1.0, 0.0)
            return tuple(accs)

        zero = jnp.zeros((SUBLANES, tq), F32)
        a0, a1, a2, a3 = lax.fori_loop(0, nk, body, (zero, zero, zero, zero))
        cnt = _rep8((a0 + a1) + (a2 + a3), jnp.sum)
        ge = cnt >= keff
        return jnp.where(ge, v, lo), jnp.where(ge, hi, v), jnp.where(ge, chi, cnt)

    lo, hi, chi = lax.fori_loop(0, COUNT_STEPS, count_body, (lo, hi, jnp.zeros((SUBLANES, tq), F32)))

    def tight_cond(state):
        it, pending = state[0], state[1]
        return jnp.logical_and(pending > 0, it < MAX_TIGHTEN_STEPS)

    def tight_body(state):
        it, _, lo, hi, chi = state
        v = pivot(lo, hi)

        def body(c, accs):
            cnt, below, above = accs
            for r in range(ck // SUBLANES):
                x = sc_ref[c, r * SUBLANES:(r + 1) * SUBLANES, :]
                gt = x > v
                cnt = cnt + jnp.where(gt, 1.0, 0.0)
                below = jnp.maximum(below, jnp.where(gt, -jnp.inf, x))
                above = jnp.minimum(above, jnp.where(gt, x, jnp.inf))
            return cnt, below, above

        cnt, below, above = lax.fori_loop(0, nk, body, (jnp.zeros((SUBLANES, tq), F32), -inf8, inf8))
        cnt = _rep8(cnt, jnp.sum)
        below = _rep8(below, jnp.max)
        above = _rep8(above, jnp.min)
        ge = cnt >= keff
        lo = jnp.where(ge, above, lo)
        chi = jnp.where(ge, chi, cnt)
        hi = jnp.where(ge, hi, below)
        pending = jnp.max(jnp.where(lo < hi, 1.0, 0.0)) > 0.5
        return it + 1, pending.astype(jnp.int32), lo, hi, chi

    state = lax.while_loop(tight_cond, tight_body, (jnp.int32(0), jnp.int32(1), lo, hi, chi))
    thr = state[3][0:1, :]
    ties_wanted = (keff - state[4])[0:1, :]

    def mask_body(c, tie_carry):
        x = sc_ref[c]
        eq = x == thr
        eqb = jnp.where(eq, 1.0, 0.0).astype(BF16)
        prefix = _dot(tri_ref[...], eqb)
        sel = jnp.logical_or(x > thr, jnp.logical_and(eq, prefix <= ties_wanted - tie_carry))
        bias_ref[c] = jnp.where(sel, 0.0, NEG_BIG).astype(BF16)
        return tie_carry + _dot(ones_ref[...], eqb)[0:1, :]

    lax.fori_loop(0, nk, mask_body, jnp.zeros((1, tq), F32))

    for hh in range(ATT_HEADS):
        qa_ref[:, hh * tq:(hh + 1) * tq] = q_ref[0, hh]
    acc_ref[...] = jnp.zeros(acc_ref.shape, F32)

    def logits(c, buf):
        off = pl.multiple_of(c * ck, ck)
        sb = _dot(k_ref[0, pl.ds(off, ck), :], qa_ref[...]).astype(BF16)
        buf[...] = sb + jnp.concatenate([bias_ref[c]] * ATT_HEADS, axis=1)

    def accumulate(c, buf, m_old):
        sb = buf[...]
        cm = jnp.max(_group_reduce(sb, jnp.maximum, 2 * SUBLANES).astype(F32), axis=0, keepdims=True)
        m_new = jnp.maximum(m_old, cm)
        p = jnp.exp2(sb - m_new.astype(BF16))
        acc_ref[...] = acc_ref[...] * jnp.exp2(m_old - m_new) + _dot(v_ref[0, c], p)
        return m_new

    def att_pair(t, m):
        logits(2 * t + 1, lg_b)
        m = accumulate(2 * t, lg_a, m)
        logits(2 * t + 2, lg_a)
        return accumulate(2 * t + 1, lg_b, m)

    def att_tail(_, m):
        logits(i, lg_b)
        return accumulate(i, lg_b, m)

    logits(0, lg_a)
    m_run = lax.fori_loop(0, i // 2, att_pair, jnp.full((1, ATT_HEADS * tq), NEG_BIG, BF16).astype(F32))
    m_run = accumulate(2 * (i // 2), lg_a, m_run)
    lax.fori_loop(0, i % 2, att_tail, m_run)
    outs = []
    for hh in range(ATT_HEADS):
        a = acc_ref[:, hh * tq:(hh + 1) * tq]
        outs.append(a[0:HEAD_DIM, :] * (1.0 / a[HEAD_DIM:HEAD_DIM + 1, :]))
    out_ref[0] = jnp.concatenate(outs, axis=0).T.astype(BF16)


def _sparse_attention(qt, k_aug, vt_ext, iqt, ik, iwt):
    bsz, _, _, s_len = qt.shape
    tq = ATT_TILE
    topk = min(TOPK_MAX, s_len // 4)
    n_chunks = s_len // tq
    idx = np.arange(tq)
    tri = jnp.asarray(idx[None, :] <= idx[:, None], BF16)
    ones = jnp.ones((SUBLANES, tq), BF16)
    heads_t = lambda n, d: pl.BlockSpec((1, n, d, tq), lambda b, j: (b, 0, 0, j))
    return pl.pallas_call(
        functools.partial(_dsa_kernel, topk=topk),
        grid=(bsz, n_chunks),
        in_specs=[heads_t(IDX_HEADS, IDX_DIM), pl.BlockSpec((1, IDX_HEADS, tq), lambda b, j: (b, 0, j)),
                  heads_t(ATT_HEADS, LANES),
                  pl.BlockSpec((1, s_len, IDX_DIM), lambda b, j: (b, 0, 0)),
                  pl.BlockSpec((1, s_len, LANES), lambda b, j: (b, 0, 0)),
                  pl.BlockSpec((1, n_chunks, LANES, tq), lambda b, j: (b, 0, 0, 0)),
                  _const_spec(tri), _const_spec(ones)],
        out_specs=pl.BlockSpec((1, tq, ATT_HEADS * HEAD_DIM), lambda b, j: (b, j, 0)),
        out_shape=jax.ShapeDtypeStruct((bsz, s_len, ATT_HEADS * HEAD_DIM), BF16),
        scratch_shapes=[
            pltpu.VMEM((n_chunks, tq, tq), F32),
            pltpu.VMEM((n_chunks, tq, tq), BF16),
            pltpu.VMEM((IDX_DIM, IDX_HEADS * tq), BF16),
            pltpu.VMEM((LANES, ATT_HEADS * tq), BF16),
            pltpu.VMEM((LANES, ATT_HEADS * tq), F32),
            pltpu.VMEM((tq, ATT_HEADS * tq), BF16),
            pltpu.VMEM((tq, ATT_HEADS * tq), BF16),
        ],
        compiler_params=pltpu.CompilerParams(dimension_semantics=("arbitrary", "arbitrary"),
                                             vmem_limit_bytes=VMEM_LIMIT),
        name="sparse_attention",
    )(iqt, iwt, qt, ik, k_aug, vt_ext, tri, ones)


def _mix_ffn_kernel(x_ref, ya_ref, yb_ref, wo_ref, fg_ref, wup_ref, wd_ref, cw_ref, cb_ref,
                    out_ref, h_ref, act_ref, carry_ref):
    tq = x_ref.shape[1]
    half = ya_ref.shape[2]

    @pl.when(pl.program_id(1) == 0)
    def _():
        carry_ref[...] = jnp.zeros(carry_ref.shape, F32)

    x1 = x_ref[0] + _dot(ya_ref[0], wo_ref[0:half, :]) + _dot(yb_ref[0], wo_ref[half:2 * half, :])
    out_ref[0] = x1
    h_ref[...] = _rmsnorm_rows(x1, fg_ref[...]).astype(BF16)
    first_row = lax.broadcasted_iota(jnp.int32, (SUBLANES, FF_CHUNK), 0) == 0

    def shift_down(a, slot):
        prev = carry_ref[slot]
        carry_ref[slot] = a[tq - SUBLANES:tq, :]
        r = pltpu.roll(a, 1, 0)
        head = jnp.where(first_row, pltpu.roll(prev, 1, 0), r[0:SUBLANES, :])
        return jnp.concatenate([head, r[SUBLANES:, :]], axis=0)

    def conv(cols, slot):
        u = _dot(h_ref[...], wup_ref[:, cols])
        cw = cw_ref[:, cols]
        inner = cw[1:2, :] * u + shift_down(cw[0:1, :] * u, 2 * slot)
        return cw[2:3, :] * u + shift_down(inner, 2 * slot + 1) + cb_ref[:, cols]

    for c in range(D_FF // FF_CHUNK):
        gate = conv(slice(c * FF_CHUNK, (c + 1) * FF_CHUNK), 2 * c)
        up = conv(slice(D_FF + c * FF_CHUNK, D_FF + (c + 1) * FF_CHUNK), 2 * c + 1)
        act_ref[:, c * FF_CHUNK:(c + 1) * FF_CHUNK] = (gate * jax.nn.sigmoid(gate) * up).astype(BF16)
    out_ref[0] += _dot(act_ref[...], wd_ref[...])


def _mix_ffn(x, ya, yb, w_out, ffn_g, w_up, conv_w, conv_b, w_down):
    bsz, s_len, _ = x.shape
    tq = min(TOK_TILE, s_len)
    half = ya.shape[2]
    n_ff = D_FF // FF_CHUNK
    consts = [w_out.astype(BF16), ffn_g.reshape(1, D_MODEL), w_up.astype(BF16), w_down.astype(BF16),
              conv_w, conv_b.reshape(1, -1)]
    tok = lambda width: pl.BlockSpec((1, tq, width), lambda b, j: (b, j, 0))
    return pl.pallas_call(
        _mix_ffn_kernel,
        grid=(bsz, s_len // tq),
        in_specs=[tok(D_MODEL), tok(half), tok(half)] + [_const_spec(c) for c in consts],
        out_specs=tok(D_MODEL),
        out_shape=jax.ShapeDtypeStruct((bsz, s_len, D_MODEL), F32),
        scratch_shapes=[
            pltpu.VMEM((tq, D_MODEL), BF16),
            pltpu.VMEM((tq, D_FF), BF16),
            pltpu.VMEM((4 * n_ff, SUBLANES, FF_CHUNK), F32),
        ],
        compiler_params=pltpu.CompilerParams(dimension_semantics=("arbitrary", "arbitrary"),
                                             vmem_limit_bytes=VMEM_LIMIT),
        name="mix_out_ffn",
    )(x, ya, yb, *consts)


def _odd_in_kernel(x_ref, g_ref, w_ref, lng_ref, lnb_ref, sgw_ref, sgb_ref, wgate_ref, bgate_ref, onorm_ref,
                   lblk_ref, oblk_ref, yc_ref, yd_ref, state_ref, obuf):
    tq = x_ref.shape[1]
    hk = GLA_HEADS * GLA_DK

    @pl.when(pl.program_id(1) == 0)
    def _():
        state_ref[...] = jnp.zeros(state_ref.shape, F32)

    h = _rmsnorm_rows(x_ref[0], g_ref[...]).astype(BF16)

    uv = _dot(h, w_ref[:, 0:2 * SG_WIDTH])
    gel = 0.5 * uv * (1.0 + lax.erf(uv * np.float32(np.sqrt(0.5))))
    u = gel[:, 0:SG_WIDTH]
    vv = gel[:, SG_WIDTH:2 * SG_WIDTH]
    mu = jnp.mean(vv, axis=-1, keepdims=True)
    xc = vv - mu
    vn = (xc * lax.rsqrt(jnp.mean(xc * xc, axis=-1, keepdims=True) + LN_EPS) * lng_ref[...] + lnb_ref[...])
    vn = vn.astype(BF16)
    low = lax.broadcasted_iota(jnp.int32, (SG_CHUNK, LANES), 1) < SG_GROUP_DIM
    zero = jnp.zeros((SG_CHUNK, LANES), BF16)
    for cc in range(tq // SG_CHUNK):
        rows = slice(cc * SG_CHUNK, (cc + 1) * SG_CHUNK)
        for p in range(SG_WIDTH // LANES):
            cols = slice(p * LANES, (p + 1) * LANES)
            tile = vn[rows, cols]
            rhs = jnp.concatenate([jnp.where(low, tile, zero), jnp.where(low, zero, tile)], axis=0)
            sp = _dot(sgw_ref[p], rhs) + sgb_ref[:, cols]
            yc_ref[0, rows, cols] = (u[rows, cols] * sp).astype(BF16)

    c0 = 2 * SG_WIDTH
    qf = _dot(h, w_ref[:, c0:c0 + hk]) * (GLA_DK ** -0.5)
    kf = _dot(h, w_ref[:, c0 + hk:c0 + 2 * hk])
    c0 += 2 * hk
    vf = _dot(h, w_ref[:, c0:c0 + GLA_HEADS * GLA_DV]).astype(BF16)
    c0 += GLA_HEADS * GLA_DV
    rr = _dot(h, w_ref[:, c0:c0 + GLA_HEADS * GLA_DV])
    c0 += GLA_HEADS * GLA_DV
    pg = _dot(h, w_ref[:, c0:c0 + LANES])
    g = jax.nn.log_sigmoid(_dot(pg.astype(BF16), wgate_ref[...]) + bgate_ref[...]) * (1.0 / GATE_NORMALIZER)
    b = _dot_hilo_left(lblk_ref[...], g)
    gtot = _dot_hilo_left(oblk_ref[...], g)
    qp = qf * jnp.exp(b)
    kp = (kf * jnp.exp(-b)).astype(BF16)
    kpp = kf * jnp.exp(gtot - b)
    dec = jnp.exp(gtot)

    lane_head = lax.broadcasted_iota(jnp.int32, (GLA_CHUNK, hk), 1) // GLA_DK
    rr_i = lax.broadcasted_iota(jnp.int32, (GLA_HEADS * GLA_CHUNK, GLA_CHUNK), 0) % GLA_CHUNK
    cc_i = lax.broadcasted_iota(jnp.int32, (GLA_HEADS * GLA_CHUNK, GLA_CHUNK), 1)
    causal = rr_i >= cc_i
    state = state_ref[...]
    for c in range(tq // GLA_CHUNK):
        rows = slice(c * GLA_CHUNK, (c + 1) * GLA_CHUNK)
        qc = qp[rows, :]
        lhs = jnp.concatenate([jnp.where(lane_head == hh, qc, 0.0) for hh in range(GLA_HEADS)], axis=0).astype(BF16)
        a = jnp.where(causal, _dot_nt(lhs, kp[rows, :]), 0.0)
        vc = vf[rows, :]
        o_intra = _dot(a.astype(BF16), vc)
        o_inter = _dot(lhs, state.astype(BF16))
        kv = _dot(kpp[rows, :].T.astype(BF16), vc)
        dcol = dec[rows, :].T[:, 0:1]
        kvd = []
        for hh in range(GLA_HEADS):
            hr = slice(hh * GLA_DK, (hh + 1) * GLA_DK)
            hc = slice(hh * GLA_DV, (hh + 1) * GLA_DV)
            obuf[rows, hc] = o_inter[hr, :] + o_intra[hr, hc]
            kvd.append(kv[hr, hc])
        state = state * dcol + jnp.concatenate(kvd, axis=0)
    state_ref[...] = state

    for hh in range(GLA_HEADS):
        hc = slice(hh * GLA_DV, (hh + 1) * GLA_DV)
        o = _rmsnorm_rows(obuf[:, hc], onorm_ref[...])
        r = rr[:, hc]
        yd_ref[0, :, hc] = (o * (r * jax.nn.sigmoid(r))).astype(BF16)


def _odd_in_proj(x, norm_g, w_in, ln_g, ln_b, sg_w, sg_b, w_gate, b_gate, o_norm):
    bsz, s_len, _ = x.shape
    tq = min(TOK_TILE, s_len)
    hk = GLA_HEADS * GLA_DK
    c_glr = 2 * SG_WIDTH + 2 * hk + GLA_HEADS * GLA_DV
    w = jnp.concatenate([w_in[:, :c_glr], w_in[:, c_glr + GLA_RANK:], w_in[:, c_glr:c_glr + GLA_RANK]], axis=1)
    w = jnp.pad(w, ((0, 0), (0, ODD_PAD - w.shape[1]))).astype(BF16)
    wm = jnp.tril(sg_w)
    sgw = wm.reshape(SG_GROUPS // 2, 2, SG_CHUNK, SG_CHUNK).transpose(0, 2, 1, 3)
    sgw = sgw.reshape(SG_GROUPS // 2, SG_CHUNK, 2 * SG_CHUNK).astype(BF16)
    sgb = jnp.repeat(sg_b.T, SG_GROUP_DIM, axis=1)
    wgate = jnp.pad(w_gate, ((0, LANES - GLA_RANK), (0, 0))).astype(BF16)
    idx = np.arange(tq)
    same = (idx[:, None] // GLA_CHUNK) == (idx[None, :] // GLA_CHUNK)
    lblk = jnp.asarray(same & (idx[None, :] <= idx[:, None]), BF16)
    oblk = jnp.asarray(same, BF16)
    consts = [norm_g.reshape(1, D_MODEL), w, ln_g.reshape(1, -1), ln_b.reshape(1, -1), sgw, sgb, wgate,
              b_gate.reshape(1, -1), o_norm.reshape(1, -1), lblk, oblk]
    tok = lambda width: pl.BlockSpec((1, tq, width), lambda b, j: (b, j, 0))
    return pl.pallas_call(
        _odd_in_kernel,
        grid=(bsz, s_len // tq),
        in_specs=[tok(D_MODEL)] + [_const_spec(c) for c in consts],
        out_specs=[tok(SG_WIDTH), tok(GLA_HEADS * GLA_DV)],
        out_shape=[jax.ShapeDtypeStruct((bsz, s_len, SG_WIDTH), BF16),
                   jax.ShapeDtypeStruct((bsz, s_len, GLA_HEADS * GLA_DV), BF16)],
        scratch_shapes=[pltpu.VMEM((hk, GLA_DV), F32), pltpu.VMEM((tq, GLA_HEADS * GLA_DV), F32)],
        compiler_params=pltpu.CompilerParams(dimension_semantics=("arbitrary", "arbitrary"),
                                             vmem_limit_bytes=VMEM_LIMIT),
        name="odd_in_proj",
    )(x, *consts)


def kernel(x, positions, ev_norm, ev_w_in, sc_conv_w, q_norm, k_norm, ev_w_out, od_norm, od_w_in, sg_ln_g, sg_ln_b,
           sg_w, sg_b, gla_w_gate, gla_b_gate, gla_o_norm, od_w_out, ffn_norm, ffn_w_up, ffn_conv_w, ffn_conv_b,
           ffn_w_down):
    depth = ffn_norm.shape[0]
    cs = _rope_tables(positions)
    for layer in range(depth):
        i = layer // 2
        if layer % 2 == 0:
            ya, q, k, v, iq, ik, iw = _even_in_proj(x, cs, ev_norm[i], ev_w_in[i], sc_conv_w[i], q_norm[i], k_norm[i])
            yb = _sparse_attention(q, k, v, iq, ik, iw)
            w_out = ev_w_out[i]
        else:
            ya, yb = _odd_in_proj(x, od_norm[i], od_w_in[i], sg_ln_g[i], sg_ln_b[i], sg_w[i], sg_b[i],
                                  gla_w_gate[i], gla_b_gate[i], gla_o_norm[i])
            w_out = od_w_out[i]
        x = _mix_ffn(x, ya, yb, w_out, ffn_norm[layer], ffn_w_up[layer], ffn_conv_w[layer], ffn_conv_b[layer],
                     ffn_w_down[layer])
    return x
```

```python
import functools

import numpy as np
import jax
import jax.numpy as jnp
from jax import lax
from jax.experimental import pallas as pl
from jax.experimental.pallas import tpu as pltpu

F32 = jnp.float32
BF16 = jnp.bfloat16

D_MODEL = 1024
SC_WIDTH = 512
CONV_WIDTH = 3
ATT_HEADS = 8
HEAD_DIM = 64
IDX_HEADS = 4
IDX_DIM = 64
TOPK_MAX = 256
ROT_DIM = HEAD_DIM // 4
ROT_HALF = ROT_DIM // 2
ROPE_THETA = 500000.0
IDX_SCALE = (IDX_HEADS * IDX_DIM) ** -0.5
ATT_SCALE = HEAD_DIM ** -0.5
SG_GROUPS = 8
SG_GROUP_DIM = 64
SG_WIDTH = 512
SG_CHUNK = 128
GLA_HEADS = 4
GLA_DK = 64
GLA_DV = 128
GLA_RANK = 16
GLA_CHUNK = 64
GATE_NORMALIZER = 16.0
D_FF = 2816
EPS = 1e-6
LN_EPS = 1e-5

EVEN_COLS = 2500
EVEN_PAD = 2560
ODD_PAD = 2688

LANES = 128
SUBLANES = 8
VMEM_LIMIT = 56 * 1024 * 1024

TOK_TILE = 512
ATT_TILE = 256
COARSE_STEPS = 12
COUNT_STEPS = 2
MAX_TIGHTEN_STEPS = 512
FF_CHUNK = 256
NEG_BIG = -1e30
LOG2E = 1.4426950408889634


def _const_spec(arr):
    nd = arr.ndim
    return pl.BlockSpec(arr.shape, lambda *_: (0,) * nd, pipeline_mode=pl.Buffered(1))


def _dot(a, b):
    return jnp.dot(a, b, preferred_element_type=F32)


def _dot_nt(a, b):
    return lax.dot_general(a, b, (((1,), (1,)), ((), ())), preferred_element_type=F32)


def _split_bf16(a):
    hi = a.astype(BF16)
    lo = (a - hi.astype(F32)).astype(BF16)
    return hi, lo


def _dot_hilo(a, b):
    hi, lo = _split_bf16(a)
    return _dot(hi, b) + _dot(lo, b)


def _dot_hilo_left(b, a):
    hi, lo = _split_bf16(a)
    return _dot(b, hi) + _dot(b, lo)


def _rmsnorm_rows(x, g):
    return x * lax.rsqrt(jnp.mean(x * x, axis=-1, keepdims=True) + EPS) * g


def _rope_kernel(invf_ref, pos_ref, cos_ref, sin_ref):
    pos = pos_ref[...]
    for f in range(ROT_HALF):
        ang = pos * invf_ref[f]
        cos_ref[f] = jnp.cos(ang)
        sin_ref[f] = jnp.sin(ang)


def _rope_tables(positions):
    bsz, s_len = positions.shape
    inv_freq = ROPE_THETA ** (-jnp.arange(0, ROT_DIM, 2, dtype=F32) / ROT_DIM)
    cos, sin = pl.pallas_call(
        _rope_kernel,
        out_shape=[jax.ShapeDtypeStruct((ROT_HALF, bsz, s_len), F32)] * 2,
        in_specs=[pl.BlockSpec(memory_space=pltpu.SMEM), pl.BlockSpec(memory_space=pltpu.VMEM)],
        out_specs=[pl.BlockSpec(memory_space=pltpu.VMEM)] * 2,
        name="rope_tables",
    )(inv_freq, positions.astype(F32))
    return jnp.concatenate([cos, sin], axis=0).transpose(1, 2, 0)


def _rope_expand_consts():
    expand = np.zeros((2 * ROT_HALF, 4 * LANES), np.float32)
    add = np.zeros((1, 4 * LANES), np.float32)
    for l in range(LANES):
        m = l % HEAD_DIM
        if m < ROT_HALF:
            expand[m, l] = 1.0
            expand[ROT_HALF + m, LANES + l] = -1.0
        elif m < ROT_DIM:
            expand[m - ROT_HALF, l] = 1.0
            expand[m, LANES + l] = 1.0
        else:
            add[0, l] = 1.0
        if l < HEAD_DIM:
            expand[:, 2 * LANES + l] = expand[:, l]
            expand[:, 3 * LANES + l] = expand[:, LANES + l]
            add[0, 2 * LANES + l] = add[0, l]
        else:
            add[0, 2 * LANES + l] = 1.0
    return expand, add


def _rope_apply(x, cos, sin_signed, first_half):
    width = x.shape[1]
    partner = jnp.where(first_half, pltpu.roll(x, width - ROT_HALF, 1), pltpu.roll(x, ROT_HALF, 1))
    return x * cos + partner * sin_signed


def _even_in_kernel(x_ref, g_ref, w_ref, cw_ref, qg_ref, kg_ref, cs_ref, ex_ref, exadd_ref, bdq_ref, bdk_ref,
                    ya_ref, q_ref, k_ref, v_ref, iq_ref, ik_ref, iw_ref, zbuf):
    tq = x_ref.shape[1]

    @pl.when(pl.program_id(1) == 0)
    def _():
        zbuf[0:SUBLANES, :] = jnp.zeros((SUBLANES, SC_WIDTH), F32)

    h = _rmsnorm_rows(x_ref[0], g_ref[...]).astype(BF16)

    p_sc = _dot(h, w_ref[:, 0:3 * SC_WIDTH])
    bg = p_sc[:, 0:SC_WIDTH]
    z = p_sc[:, SC_WIDTH:2 * SC_WIDTH] * p_sc[:, 2 * SC_WIDTH:3 * SC_WIDTH]
    zbuf[SUBLANES:SUBLANES + tq, :] = z
    cw = cw_ref[...]
    conv = (cw[2:3, :] * z + cw[1:2, :] * zbuf[SUBLANES - 1:SUBLANES - 1 + tq, :]
            + cw[0:1, :] * zbuf[SUBLANES - 2:SUBLANES - 2 + tq, :])
    ya_ref[0] = (bg * conv).astype(BF16)
    zbuf[0:SUBLANES, :] = zbuf[tq:tq + SUBLANES, :]

    tab = _dot_hilo(cs_ref[0], ex_ref[...]) + exadd_ref[...]
    cos_p, sin_p = tab[:, 0:LANES], tab[:, LANES:2 * LANES]
    cos_k, sin_k = tab[:, 2 * LANES:3 * LANES], tab[:, 3 * LANES:4 * LANES]
    lane = lax.broadcasted_iota(jnp.int32, (tq, LANES), 1)
    first_half = (lane % HEAD_DIM) < ROT_HALF

    c0 = 3 * SC_WIDTH
    pq = _dot(h, w_ref[:, c0:c0 + ATT_HEADS * HEAD_DIM])
    ms = _dot_hilo(pq * pq, bdq_ref[...])
    qn = pq * lax.rsqrt(ms + EPS) * qg_ref[...]
    qr = _rope_apply(qn, jnp.concatenate([cos_p] * 4, axis=1), jnp.concatenate([sin_p] * 4, axis=1),
                     jnp.concatenate([first_half] * 4, axis=1)) * (ATT_SCALE * LOG2E)
    qt = qr.T.astype(BF16)
    pad = jnp.zeros((LANES - HEAD_DIM, tq), BF16)
    for hh in range(ATT_HEADS):
        q_ref[0, hh, 0:HEAD_DIM, :] = qt[hh * HEAD_DIM:(hh + 1) * HEAD_DIM, :]
        q_ref[0, hh, HEAD_DIM:LANES, :] = pad

    c0 += ATT_HEADS * HEAD_DIM
    pkv = _dot(h, w_ref[:, c0:c0 + LANES])
    ms_k = _dot_hilo(pkv * pkv, bdk_ref[...])
    is_key = lane < HEAD_DIM
    kvn = pkv * jnp.where(is_key, lax.rsqrt(ms_k + EPS) * kg_ref[...], 1.0)
    kvr = _rope_apply(kvn, cos_k, sin_k, first_half)
    k_ref[0] = jnp.where(is_key, kvr, jnp.where(lane == HEAD_DIM, 1.0, 0.0)).astype(BF16)
    vt = jnp.concatenate([kvr.T[HEAD_DIM:LANES, :], jnp.ones((LANES - HEAD_DIM, tq), F32)], axis=0).astype(BF16)
    for cc in range(tq // ATT_TILE):
        v_ref[0, cc] = vt[:, cc * ATT_TILE:(cc + 1) * ATT_TILE]

    c0 += LANES
    piq = _dot(h, w_ref[:, c0:c0 + IDX_HEADS * IDX_DIM])
    iqr = _rope_apply(piq, jnp.concatenate([cos_p] * 2, axis=1), jnp.concatenate([sin_p] * 2, axis=1),
                      jnp.concatenate([first_half] * 2, axis=1))
    iqt = iqr.T.astype(BF16)
    for hh in range(IDX_HEADS):
        iq_ref[0, hh] = iqt[hh * IDX_DIM:(hh + 1) * IDX_DIM, :]

    c0 += IDX_HEADS * IDX_DIM
    pik = _dot(h, w_ref[:, c0:c0 + LANES])
    ikr = _rope_apply(pik, cos_k, sin_k, first_half)
    ik_ref[0] = ikr[:, 0:IDX_DIM].astype(BF16)
    iw_ref[0] = ikr.T[IDX_DIM:IDX_DIM + IDX_HEADS, :] * IDX_SCALE


def _even_in_proj(x, cs, norm_g, w_in, conv_w, q_g, k_g):
    bsz, s_len, _ = x.shape
    tq = min(TOK_TILE, s_len)
    w = jnp.pad(w_in, ((0, 0), (0, EVEN_PAD - EVEN_COLS))).astype(BF16)
    expand, add = _rope_expand_consts()
    hd = np.arange(ATT_HEADS * HEAD_DIM) // HEAD_DIM
    bdq = jnp.asarray((hd[:, None] == hd[None, :]) / HEAD_DIM, BF16)
    kk = np.arange(LANES)
    bdk = jnp.asarray(((kk[:, None] < HEAD_DIM) & (kk[None, :] < HEAD_DIM)) / HEAD_DIM, BF16)
    consts = [
        norm_g.reshape(1, D_MODEL), w, conv_w,
        jnp.tile(q_g, ATT_HEADS).reshape(1, -1),
        jnp.concatenate([k_g, jnp.ones((LANES - HEAD_DIM,), F32)]).reshape(1, LANES),
    ]
    consts2 = [jnp.asarray(expand, BF16), jnp.asarray(add, F32), bdq, bdk]
    tok = lambda width: pl.BlockSpec((1, tq, width), lambda b, j: (b, j, 0))
    per_tile = tq // ATT_TILE
    heads_t = lambda n, d: pl.BlockSpec((1, n, d, tq), lambda b, j: (b, 0, 0, j))
    return pl.pallas_call(
        _even_in_kernel,
        grid=(bsz, s_len // tq),
        in_specs=[tok(D_MODEL)] + [_const_spec(c) for c in consts] + [tok(2 * ROT_HALF)]
                 + [_const_spec(c) for c in consts2],
        out_specs=[tok(SC_WIDTH), heads_t(ATT_HEADS, LANES), tok(LANES),
                   pl.BlockSpec((1, per_tile, LANES, ATT_TILE), lambda b, j: (b, j, 0, 0)),
                   heads_t(IDX_HEADS, IDX_DIM), tok(IDX_DIM),
                   pl.BlockSpec((1, IDX_HEADS, tq), lambda b, j: (b, 0, j))],
        out_shape=[
            jax.ShapeDtypeStruct((bsz, s_len, SC_WIDTH), BF16),
            jax.ShapeDtypeStruct((bsz, ATT_HEADS, LANES, s_len), BF16),
            jax.ShapeDtypeStruct((bsz, s_len, LANES), BF16),
            jax.ShapeDtypeStruct((bsz, s_len // ATT_TILE, LANES, ATT_TILE), BF16),
            jax.ShapeDtypeStruct((bsz, IDX_HEADS, IDX_DIM, s_len), BF16),
            jax.ShapeDtypeStruct((bsz, s_len, IDX_DIM), BF16),
            jax.ShapeDtypeStruct((bsz, IDX_HEADS, s_len), F32),
        ],
        scratch_shapes=[pltpu.VMEM((tq + SUBLANES, SC_WIDTH), F32)],
        compiler_params=pltpu.CompilerParams(dimension_semantics=("arbitrary", "arbitrary"),
                                             vmem_limit_bytes=VMEM_LIMIT),
        name="even_in_proj",
    )(x, *consts, cs, *consts2)


def _group_reduce(x, op, group=SUBLANES):
    parts = [x[r:r + group, :] for r in range(0, x.shape[0], group)]
    while len(parts) > 1:
        nxt = [op(parts[a], parts[a + 1]) for a in range(0, len(parts) - 1, 2)]
        if len(parts) % 2:
            nxt.append(parts[-1])
        parts = nxt
    return parts[0]


def _rep8(x8, op):
    return jnp.broadcast_to(op(x8, axis=0, keepdims=True), x8.shape)


def _ceil_bf16(x):
    bits = lax.bitcast_convert_type(x, jnp.int32)
    bits = jnp.where(bits >= 0, bits + 0xFFFF, bits)
    return lax.bitcast_convert_type(bits & jnp.int32(-65536), F32).astype(BF16)


def _dsa_kernel(iq_ref, iw_ref, q_ref, ik_ref, k_ref, v_ref, tri_ref, out_ref,
                sc_ref, scb_ref, iqa_ref, qa_ref, acc_ref, lg_a, lg_b, *, topk):
    tq = ATT_TILE
    ck = ATT_TILE
    i = pl.program_id(1)
    nk = i + 1
    qpos = i * tq + lax.broadcasted_iota(jnp.int32, (SUBLANES, tq), 1)
    keff = jnp.minimum(qpos + 1, topk).astype(F32)
    iw = iw_ref[0]
    inf8 = jnp.full((SUBLANES, tq), jnp.inf, F32)

    for hh in range(IDX_HEADS):
        iqa_ref[:, hh * tq:(hh + 1) * tq] = iq_ref[0, hh]

    def scores(c):
        off = pl.multiple_of(c * ck, ck)
        z = _dot(ik_ref[0, pl.ds(off, ck), :], iqa_ref[...])
        acc = jnp.zeros((ck, tq), F32)
        for hh in range(IDX_HEADS):
            acc = acc + jnp.maximum(z[:, hh * tq:(hh + 1) * tq], 0.0) * iw[hh:hh + 1, :]
        return acc

    def score_chunk(c, carry):
        lo, hi = carry
        s = scores(c)
        sc_ref[c] = s
        scb_ref[c] = _ceil_bf16(s)
        return jnp.minimum(lo, _group_reduce(s, jnp.minimum)), jnp.maximum(hi, _group_reduce(s, jnp.maximum))

    lo, hi = lax.fori_loop(0, i // 2, lambda t, cr: score_chunk(2 * t + 1, score_chunk(2 * t, cr)), (inf8, -inf8))
    lo, hi = lax.fori_loop(0, i % 2, lambda _, cr: score_chunk(i - 1, cr), (lo, hi))
    s = scores(i)
    kpos = i * ck + lax.broadcasted_iota(jnp.int32, (ck, tq), 0)
    valid = kpos <= i * tq + lax.broadcasted_iota(jnp.int32, (ck, tq), 1)
    sc_ref[i] = jnp.where(valid, s, -jnp.inf)
    scb_ref[i] = _ceil_bf16(jnp.where(valid, s, -jnp.inf))
    lo = _rep8(jnp.minimum(lo, _group_reduce(jnp.where(valid, s, jnp.inf), jnp.minimum)), jnp.min)
    hi = _rep8(jnp.maximum(hi, _group_reduce(jnp.where(valid, s, -jnp.inf), jnp.maximum)), jnp.max)

    def pivot(lo, hi):
        mid = 0.5 * lo + 0.5 * hi
        return jnp.where(mid < hi, jnp.maximum(mid, lo), lo)

    def coarse_body(_, carry):
        lo, hi, chi = carry
        v = pivot(lo, hi).astype(BF16).astype(F32)
        vb = jnp.concatenate([v, v], axis=0).astype(BF16)
        one = jnp.ones((2 * SUBLANES, tq), BF16)
        zero = jnp.zeros((2 * SUBLANES, tq), BF16)

        def body(c, accs):
            accs = list(accs)
            for r in range(ck // (2 * SUBLANES)):
                x = scb_ref[c, r * 2 * SUBLANES:(r + 1) * 2 * SUBLANES, :]
                accs[r % 4] = accs[r % 4] + jnp.where(x > vb, one, zero)
            return tuple(accs)

        a0, a1, a2, a3 = lax.fori_loop(0, nk, body, (zero, zero, zero, zero))
        cnt = jnp.sum(((a0 + a1) + (a2 + a3)).astype(F32), axis=0, keepdims=True)
        inside = jnp.logical_and(v >= lo, v < hi)
        up = jnp.logical_and(inside, cnt >= keff)
        down = jnp.logical_and(inside, cnt < keff)
        return jnp.where(up, v, lo), jnp.where(down, v, hi), jnp.where(down, cnt, chi)

    lo, hi, chi = lax.fori_loop(0, COARSE_STEPS, coarse_body, (lo, hi, jnp.zeros((SUBLANES, tq), F32)))

    def count_body(_, carry):
        lo, hi, chi = carry
        v = pivot(lo, hi)

        def body(c, accs):
            accs = list(accs)
            for r in range(ck // SUBLANES):
                x = sc_ref[c, r * SUBLANES:(r + 1) * SUBLANES, :]
                accs[r % 4] = accs[r % 4] + jnp.where(x > v, 1.0, 0.0)
            return tuple(accs)

        zero = jnp.zeros((SUBLANES, tq), F32)
        a0, a1, a2, a3 = lax.fori_loop(0, nk, body, (zero, zero, zero, zero))
        cnt = _rep8((a0 + a1) + (a2 + a3), jnp.sum)
        ge = cnt >= keff
        return jnp.where(ge, v, lo), jnp.where(ge, hi, v), jnp.where(ge, chi, cnt)

    lo, hi, chi = lax.fori_loop(0, COUNT_STEPS, count_body, (lo, hi, chi))

    def tight_cond(state):
        it, pending = state[0], state[1]
        return jnp.logical_and(pending > 0, it < MAX_TIGHTEN_STEPS)

    def tight_body(state):
        it, _, lo, hi, chi = state
        v = pivot(lo, hi)

        def body(c, accs):
            cnt, below, above = accs
            for r in range(ck // SUBLANES):
                x = sc_ref[c, r * SUBLANES:(r + 1) * SUBLANES, :]
                gt = x > v
                cnt = cnt + jnp.where(gt, 1.0, 0.0)
                below = jnp.maximum(below, jnp.where(gt, -jnp.inf, x))
                above = jnp.minimum(above, jnp.where(gt, x, jnp.inf))
            return cnt, below, above

        cnt, below, above = lax.fori_loop(0, nk, body, (jnp.zeros((SUBLANES, tq), F32), -inf8, inf8))
        cnt = _rep8(cnt, jnp.sum)
        below = _rep8(below, jnp.max)
        above = _rep8(above, jnp.min)
        ge = cnt >= keff
        lo = jnp.where(ge, above, lo)
        chi = jnp.where(ge, chi, cnt)
        hi = jnp.where(ge, hi, below)
        pending = jnp.max(jnp.where(lo < hi, 1.0, 0.0)) > 0.5
        return it + 1, pending.astype(jnp.int32), lo, hi, chi

    state = lax.while_loop(tight_cond, tight_body, (jnp.int32(0), jnp.int32(1), lo, hi, chi))
    thr = state[3][0:1, :]
    ties_wanted = (keff - state[4])[0:1, :]

    for hh in range(ATT_HEADS):
        qa_ref[:, hh * tq:(hh + 1) * tq] = q_ref[0, hh]
    acc_ref[...] = jnp.zeros(acc_ref.shape, F32)

    def logits(c, buf, ties_seen):
        x = sc_ref[c]
        eq = x == thr
        prefix = _dot(tri_ref[...], jnp.where(eq, 1.0, 0.0).astype(BF16))
        sel = jnp.logical_or(x > thr, jnp.logical_and(eq, prefix <= ties_wanted - ties_seen))
        bias = jnp.where(sel, 0.0, NEG_BIG).astype(BF16)
        off = pl.multiple_of(c * ck, ck)
        sb = _dot(k_ref[0, pl.ds(off, ck), :], qa_ref[...]).astype(BF16)
        buf[...] = sb + jnp.concatenate([bias] * ATT_HEADS, axis=1)
        return ties_seen + prefix[ck - 1:ck, :]

    def accumulate(c, buf, m_old):
        sb = buf[...]
        cm = jnp.max(_group_reduce(sb, jnp.maximum, 2 * SUBLANES).astype(F32), axis=0, keepdims=True)
        m_new = jnp.maximum(m_old, cm)
        p = jnp.exp2(sb - m_new.astype(BF16))
        acc_ref[...] = acc_ref[...] * jnp.exp2(m_old - m_new) + _dot(v_ref[0, c], p)
        return m_new

    def att_pair(t, carry):
        m, ties = carry
        ties = logits(2 * t + 1, lg_b, ties)
        m = accumulate(2 * t, lg_a, m)
        ties = logits(2 * t + 2, lg_a, ties)
        return accumulate(2 * t + 1, lg_b, m), ties

    def att_tail(_, carry):
        m, ties = carry
        ties = logits(i, lg_b, ties)
        return accumulate(i, lg_b, m), ties

    ties = logits(0, lg_a, jnp.zeros((1, tq), F32))
    m_run = jnp.full((1, ATT_HEADS * tq), NEG_BIG, BF16).astype(F32)
    m_run, ties = lax.fori_loop(0, i // 2, att_pair, (m_run, ties))
    m_run = accumulate(2 * (i // 2), lg_a, m_run)
    lax.fori_loop(0, i % 2, att_tail, (m_run, ties))
    outs = []
    for hh in range(ATT_HEADS):
        a = acc_ref[:, hh * tq:(hh + 1) * tq]
        outs.append(a[0:HEAD_DIM, :] * (1.0 / a[HEAD_DIM:HEAD_DIM + 1, :]))
    out_ref[0] = jnp.concatenate(outs, axis=0).T.astype(BF16)


def _sparse_attention(qt, k_aug, vt_ext, iqt, ik, iwt):
    bsz, _, _, s_len = qt.shape
    tq = ATT_TILE
    topk = min(TOPK_MAX, s_len // 4)
    n_chunks = s_len // tq
    idx = np.arange(tq)
    tri = jnp.asarray(idx[None, :] <= idx[:, None], BF16)
    assert n_chunks * tq // (2 * SUBLANES) <= 256
    heads_t = lambda n, d: pl.BlockSpec((1, n, d, tq), lambda b, j: (b, 0, 0, j))
    return pl.pallas_call(
        functools.partial(_dsa_kernel, topk=topk),
        grid=(bsz, n_chunks),
        in_specs=[heads_t(IDX_HEADS, IDX_DIM), pl.BlockSpec((1, IDX_HEADS, tq), lambda b, j: (b, 0, j)),
                  heads_t(ATT_HEADS, LANES),
                  pl.BlockSpec((1, s_len, IDX_DIM), lambda b, j: (b, 0, 0)),
                  pl.BlockSpec((1, s_len, LANES), lambda b, j: (b, 0, 0)),
                  pl.BlockSpec((1, n_chunks, LANES, tq), lambda b, j: (b, 0, 0, 0)),
                  _const_spec(tri)],
        out_specs=pl.BlockSpec((1, tq, ATT_HEADS * HEAD_DIM), lambda b, j: (b, j, 0)),
        out_shape=jax.ShapeDtypeStruct((bsz, s_len, ATT_HEADS * HEAD_DIM), BF16),
        scratch_shapes=[
            pltpu.VMEM((n_chunks, tq, tq), F32),
            pltpu.VMEM((n_chunks, tq, tq), BF16),
            pltpu.VMEM((IDX_DIM, IDX_HEADS * tq), BF16),
            pltpu.VMEM((LANES, ATT_HEADS * tq), BF16),
            pltpu.VMEM((LANES, ATT_HEADS * tq), F32),
            pltpu.VMEM((tq, ATT_HEADS * tq), BF16),
            pltpu.VMEM((tq, ATT_HEADS * tq), BF16),
        ],
        compiler_params=pltpu.CompilerParams(dimension_semantics=("arbitrary", "arbitrary"),
                                             vmem_limit_bytes=VMEM_LIMIT),
        name="sparse_attention",
    )(iqt, iwt, qt, ik, k_aug, vt_ext, tri)


def _mix_ffn_kernel(x_ref, ya_ref, yb_ref, wo_ref, fg_ref, wup_ref, wd_ref, cw_ref, cb_ref,
                    out_ref, h_ref, act_ref, carry_ref):
    tq = x_ref.shape[1]
    half = ya_ref.shape[2]

    @pl.when(pl.program_id(1) == 0)
    def _():
        carry_ref[...] = jnp.zeros(carry_ref.shape, F32)

    x1 = x_ref[0] + _dot(ya_ref[0], wo_ref[0:half, :]) + _dot(yb_ref[0], wo_ref[half:2 * half, :])
    out_ref[0] = x1
    h_ref[...] = _rmsnorm_rows(x1, fg_ref[...]).astype(BF16)
    first_row = lax.broadcasted_iota(jnp.int32, (SUBLANES, FF_CHUNK), 0) == 0

    def shift_down(a, slot):
        prev = carry_ref[slot]
        carry_ref[slot] = a[tq - SUBLANES:tq, :]
        r = pltpu.roll(a, 1, 0)
        head = jnp.where(first_row, pltpu.roll(prev, 1, 0), r[0:SUBLANES, :])
        return jnp.concatenate([head, r[SUBLANES:, :]], axis=0)

    def conv(cols, slot):
        u = _dot(h_ref[...], wup_ref[:, cols])
        cw = cw_ref[:, cols]
        inner = cw[1:2, :] * u + shift_down(cw[0:1, :] * u, 2 * slot)
        return cw[2:3, :] * u + shift_down(inner, 2 * slot + 1) + cb_ref[:, cols]

    for c in range(D_FF // FF_CHUNK):
        gate = conv(slice(c * FF_CHUNK, (c + 1) * FF_CHUNK), 2 * c)
        up = conv(slice(D_FF + c * FF_CHUNK, D_FF + (c + 1) * FF_CHUNK), 2 * c + 1)
        act_ref[:, c * FF_CHUNK:(c + 1) * FF_CHUNK] = (gate * jax.nn.sigmoid(gate) * up).astype(BF16)
    out_ref[0] += _dot(act_ref[...], wd_ref[...])


def _mix_ffn(x, ya, yb, w_out, ffn_g, w_up, conv_w, conv_b, w_down):
    bsz, s_len, _ = x.shape
    tq = min(TOK_TILE, s_len)
    half = ya.shape[2]
    n_ff = D_FF // FF_CHUNK
    consts = [w_out.astype(BF16), ffn_g.reshape(1, D_MODEL), w_up.astype(BF16), w_down.astype(BF16),
              conv_w, conv_b.reshape(1, -1)]
    tok = lambda width: pl.BlockSpec((1, tq, width), lambda b, j: (b, j, 0))
    return pl.pallas_call(
        _mix_ffn_kernel,
        grid=(bsz, s_len // tq),
        in_specs=[tok(D_MODEL), tok(half), tok(half)] + [_const_spec(c) for c in consts],
        out_specs=tok(D_MODEL),
        out_shape=jax.ShapeDtypeStruct((bsz, s_len, D_MODEL), F32),
        scratch_shapes=[
            pltpu.VMEM((tq, D_MODEL), BF16),
            pltpu.VMEM((tq, D_FF), BF16),
            pltpu.VMEM((4 * n_ff, SUBLANES, FF_CHUNK), F32),
        ],
        compiler_params=pltpu.CompilerParams(dimension_semantics=("arbitrary", "arbitrary"),
                                             vmem_limit_bytes=VMEM_LIMIT),
        name="mix_out_ffn",
    )(x, ya, yb, *consts)


def _odd_in_kernel(x_ref, g_ref, w_ref, lng_ref, lnb_ref, sgw_ref, sgb_ref, wgate_ref, bgate_ref, onorm_ref,
                   tril_ref, yc_ref, yd_ref, state_ref, obuf):
    tq = x_ref.shape[1]
    hk = GLA_HEADS * GLA_DK

    @pl.when(pl.program_id(1) == 0)
    def _():
        state_ref[...] = jnp.zeros(state_ref.shape, F32)

    h = _rmsnorm_rows(x_ref[0], g_ref[...]).astype(BF16)

    uv = _dot(h, w_ref[:, 0:2 * SG_WIDTH])
    gel = 0.5 * uv * (1.0 + lax.erf(uv * np.float32(np.sqrt(0.5))))
    u = gel[:, 0:SG_WIDTH]
    vv = gel[:, SG_WIDTH:2 * SG_WIDTH]
    mu = jnp.mean(vv, axis=-1, keepdims=True)
    xc = vv - mu
    vn = (xc * lax.rsqrt(jnp.mean(xc * xc, axis=-1, keepdims=True) + LN_EPS) * lng_ref[...] + lnb_ref[...])
    vn = vn.astype(BF16)
    low = lax.broadcasted_iota(jnp.int32, (SG_CHUNK, LANES), 1) < SG_GROUP_DIM
    zero = jnp.zeros((SG_CHUNK, LANES), BF16)
    for cc in range(tq // SG_CHUNK):
        rows = slice(cc * SG_CHUNK, (cc + 1) * SG_CHUNK)
        for p in range(SG_WIDTH // LANES):
            cols = slice(p * LANES, (p + 1) * LANES)
            tile = vn[rows, cols]
            rhs = jnp.concatenate([jnp.where(low, tile, zero), jnp.where(low, zero, tile)], axis=0)
            sp = _dot(sgw_ref[p], rhs) + sgb_ref[:, cols]
            yc_ref[0, rows, cols] = (u[rows, cols] * sp).astype(BF16)

    c0 = 2 * SG_WIDTH
    qf = _dot(h, w_ref[:, c0:c0 + hk]) * (GLA_DK ** -0.5)
    kf = _dot(h, w_ref[:, c0 + hk:c0 + 2 * hk])
    c0 += 2 * hk
    vf = _dot(h, w_ref[:, c0:c0 + GLA_HEADS * GLA_DV]).astype(BF16)
    c0 += GLA_HEADS * GLA_DV
    rr = _dot(h, w_ref[:, c0:c0 + GLA_HEADS * GLA_DV])
    c0 += GLA_HEADS * GLA_DV
    pg = _dot(h, w_ref[:, c0:c0 + LANES])
    g = jax.nn.log_sigmoid(_dot(pg.astype(BF16), wgate_ref[...]) + bgate_ref[...]) * (1.0 / GATE_NORMALIZER)
    g_hi, g_lo = _split_bf16(g)
    b_parts, tot_parts = [], []
    for c in range(tq // GLA_CHUNK):
        rows = slice(c * GLA_CHUNK, (c + 1) * GLA_CHUNK)
        bc = _dot(tril_ref[...], g_hi[rows, :]) + _dot(tril_ref[...], g_lo[rows, :])
        b_parts.append(bc)
        tot_parts.append(jnp.broadcast_to(bc[GLA_CHUNK - 1:GLA_CHUNK, :], bc.shape))
    b = jnp.concatenate(b_parts, axis=0)
    gtot = jnp.concatenate(tot_parts, axis=0)
    qp = qf * jnp.exp(b)
    kp = (kf * jnp.exp(-b)).astype(BF16)
    kpp = kf * jnp.exp(gtot - b)
    dec = jnp.exp(gtot)

    lane_head = lax.broadcasted_iota(jnp.int32, (GLA_CHUNK, hk), 1) // GLA_DK
    rr_i = lax.broadcasted_iota(jnp.int32, (GLA_HEADS * GLA_CHUNK, GLA_CHUNK), 0) % GLA_CHUNK
    cc_i = lax.broadcasted_iota(jnp.int32, (GLA_HEADS * GLA_CHUNK, GLA_CHUNK), 1)
    causal = rr_i >= cc_i
    state = state_ref[...]
    for c in range(tq // GLA_CHUNK):
        rows = slice(c * GLA_CHUNK, (c + 1) * GLA_CHUNK)
        qc = qp[rows, :]
        lhs = jnp.concatenate([jnp.where(lane_head == hh, qc, 0.0) for hh in range(GLA_HEADS)], axis=0).astype(BF16)
        a = jnp.where(causal, _dot_nt(lhs, kp[rows, :]), 0.0)
        vc = vf[rows, :]
        a = a.astype(BF16)
        o_inter = _dot(lhs, state.astype(BF16))
        kt = kpp[rows, :].T.astype(BF16)
        dcol = dec[rows, :].T[:, 0:1]
        kvd = []
        for hh in range(GLA_HEADS):
            hr = slice(hh * GLA_DK, (hh + 1) * GLA_DK)
            hc = slice(hh * GLA_DV, (hh + 1) * GLA_DV)
            obuf[rows, hc] = o_inter[hr, :] + _dot(a[hr, :], vc[:, hc])
            kvd.append(_dot(kt[hr, :], vc[:, hc]))
        state = state * dcol + jnp.concatenate(kvd, axis=0)
    state_ref[...] = state

    for hh in range(GLA_HEADS):
        hc = slice(hh * GLA_DV, (hh + 1) * GLA_DV)
        o = _rmsnorm_rows(obuf[:, hc], onorm_ref[...])
        r = rr[:, hc]
        yd_ref[0, :, hc] = (o * (r * jax.nn.sigmoid(r))).astype(BF16)


def _odd_in_proj(x, norm_g, w_in, ln_g, ln_b, sg_w, sg_b, w_gate, b_gate, o_norm):
    bsz, s_len, _ = x.shape
    tq = min(TOK_TILE, s_len)
    hk = GLA_HEADS * GLA_DK
    c_glr = 2 * SG_WIDTH + 2 * hk + GLA_HEADS * GLA_DV
    w = jnp.concatenate([w_in[:, :c_glr], w_in[:, c_glr + GLA_RANK:], w_in[:, c_glr:c_glr + GLA_RANK]], axis=1)
    w = jnp.pad(w, ((0, 0), (0, ODD_PAD - w.shape[1]))).astype(BF16)
    wm = jnp.tril(sg_w)
    sgw = wm.reshape(SG_GROUPS // 2, 2, SG_CHUNK, SG_CHUNK).transpose(0, 2, 1, 3)
    sgw = sgw.reshape(SG_GROUPS // 2, SG_CHUNK, 2 * SG_CHUNK).astype(BF16)
    sgb = jnp.repeat(sg_b.T, SG_GROUP_DIM, axis=1)
    wgate = jnp.pad(w_gate, ((0, LANES - GLA_RANK), (0, 0))).astype(BF16)
    idx = np.arange(GLA_CHUNK)
    tril = jnp.asarray(idx[None, :] <= idx[:, None], BF16)
    consts = [norm_g.reshape(1, D_MODEL), w, ln_g.reshape(1, -1), ln_b.reshape(1, -1), sgw, sgb, wgate,
              b_gate.reshape(1, -1), o_norm.reshape(1, -1), tril]
    tok = lambda width: pl.BlockSpec((1, tq, width), lambda b, j: (b, j, 0))
    return pl.pallas_call(
        _odd_in_kernel,
        grid=(bsz, s_len // tq),
        in_specs=[tok(D_MODEL)] + [_const_spec(c) for c in consts],
        out_specs=[tok(SG_WIDTH), tok(GLA_HEADS * GLA_DV)],
        out_shape=[jax.ShapeDtypeStruct((bsz, s_len, SG_WIDTH), BF16),
                   jax.ShapeDtypeStruct((bsz, s_len, GLA_HEADS * GLA_DV), BF16)],
        scratch_shapes=[pltpu.VMEM((hk, GLA_DV), F32), pltpu.VMEM((tq, GLA_HEADS * GLA_DV), F32)],
        compiler_params=pltpu.CompilerParams(dimension_semantics=("arbitrary", "arbitrary"),
                                             vmem_limit_bytes=VMEM_LIMIT),
        name="odd_in_proj",
    )(x, *consts)


def kernel(x, positions, ev_norm, ev_w_in, sc_conv_w, q_norm, k_norm, ev_w_out, od_norm, od_w_in, sg_ln_g, sg_ln_b,
           sg_w, sg_b, gla_w_gate, gla_b_gate, gla_o_norm, od_w_out, ffn_norm, ffn_w_up, ffn_conv_w, ffn_conv_b,
           ffn_w_down):
    depth = ffn_norm.shape[0]
    cs = _rope_tables(positions)
    for layer in range(depth):
        i = layer // 2
        if layer % 2 == 0:
            ya, q, k, v, iq, ik, iw = _even_in_proj(x, cs, ev_norm[i], ev_w_in[i], sc_conv_w[i], q_norm[i], k_norm[i])
            yb = _sparse_attention(q, k, v, iq, ik, iw)
            w_out = ev_w_out[i]
        else:
            ya, yb = _odd_in_proj(x, od_norm[i], od_w_in[i], sg_ln_g[i], sg_ln_b[i], sg_w[i], sg_b[i],
                                  gla_w_gate[i], gla_b_gate[i], gla_o_norm[i])
            w_out = od_w_out[i]
        x = _mix_ffn(x, ya, yb, w_out, ffn_norm[layer], ffn_w_up[layer], ffn_conv_w[layer], ffn_conv_b[layer],
                     ffn_w_down[layer])
    return x
```

```python
import functools

import numpy as np
import jax
import jax.numpy as jnp
from jax import lax
from jax.experimental import pallas as pl
from jax.experimental.pallas import tpu as pltpu

F32 = jnp.float32
BF16 = jnp.bfloat16

D_MODEL = 1024
SC_WIDTH = 512
CONV_WIDTH = 3
ATT_HEADS = 8
HEAD_DIM = 64
IDX_HEADS = 4
IDX_DIM = 64
TOPK_MAX = 256
ROT_DIM = HEAD_DIM // 4
ROT_HALF = ROT_DIM // 2
ROPE_THETA = 500000.0
IDX_SCALE = (IDX_HEADS * IDX_DIM) ** -0.5
ATT_SCALE = HEAD_DIM ** -0.5
SG_GROUPS = 8
SG_GROUP_DIM = 64
SG_WIDTH = 512
SG_CHUNK = 128
GLA_HEADS = 4
GLA_DK = 64
GLA_DV = 128
GLA_RANK = 16
GLA_CHUNK = 64
GATE_NORMALIZER = 16.0
D_FF = 2816
EPS = 1e-6
LN_EPS = 1e-5

EVEN_COLS = 2500
EVEN_PAD = 2560
ODD_PAD = 2688

LANES = 128
SUBLANES = 8
VMEM_LIMIT = 56 * 1024 * 1024

TOK_TILE = 512
FFN_TILE = 1024
ATT_TILE = 256
COARSE_STEPS = 12
COUNT_STEPS = 2
MAX_TIGHTEN_STEPS = 512
FF_CHUNK = 256
NEG_BIG = -1e30
LOG2E = 1.4426950408889634


def _const_spec(arr):
    nd = arr.ndim
    return pl.BlockSpec(arr.shape, lambda *_: (0,) * nd, pipeline_mode=pl.Buffered(1))


def _dot(a, b):
    return jnp.dot(a, b, preferred_element_type=F32)


def _dot_nt(a, b):
    return lax.dot_general(a, b, (((1,), (1,)), ((), ())), preferred_element_type=F32)


def _split_bf16(a):
    hi = a.astype(BF16)
    lo = (a - hi.astype(F32)).astype(BF16)
    return hi, lo


def _dot_hilo(a, b):
    hi, lo = _split_bf16(a)
    return _dot(hi, b) + _dot(lo, b)


def _dot_hilo_left(b, a):
    hi, lo = _split_bf16(a)
    return _dot(b, hi) + _dot(b, lo)


def _rmsnorm_rows(x, g):
    return x * lax.rsqrt(jnp.mean(x * x, axis=-1, keepdims=True) + EPS) * g


def _rope_kernel(invf_ref, pos_ref, cos_ref, sin_ref):
    pos = pos_ref[...]
    for f in range(ROT_HALF):
        ang = pos * invf_ref[f]
        cos_ref[f] = jnp.cos(ang)
        sin_ref[f] = jnp.sin(ang)


def _rope_tables(positions):
    bsz, s_len = positions.shape
    inv_freq = ROPE_THETA ** (-jnp.arange(0, ROT_DIM, 2, dtype=F32) / ROT_DIM)
    cos, sin = pl.pallas_call(
        _rope_kernel,
        out_shape=[jax.ShapeDtypeStruct((ROT_HALF, bsz, s_len), F32)] * 2,
        in_specs=[pl.BlockSpec(memory_space=pltpu.SMEM), pl.BlockSpec(memory_space=pltpu.VMEM)],
        out_specs=[pl.BlockSpec(memory_space=pltpu.VMEM)] * 2,
        name="rope_tables",
    )(inv_freq, positions.astype(F32))
    return jnp.concatenate([cos, sin], axis=0).transpose(1, 2, 0)


def _rope_expand_consts():
    expand = np.zeros((2 * ROT_HALF, 4 * LANES), np.float32)
    add = np.zeros((1, 4 * LANES), np.float32)
    for l in range(LANES):
        m = l % HEAD_DIM
        if m < ROT_HALF:
            expand[m, l] = 1.0
            expand[ROT_HALF + m, LANES + l] = -1.0
        elif m < ROT_DIM:
            expand[m - ROT_HALF, l] = 1.0
            expand[m, LANES + l] = 1.0
        else:
            add[0, l] = 1.0
        if l < HEAD_DIM:
            expand[:, 2 * LANES + l] = expand[:, l]
            expand[:, 3 * LANES + l] = expand[:, LANES + l]
            add[0, 2 * LANES + l] = add[0, l]
        else:
            add[0, 2 * LANES + l] = 1.0
    return expand, add


def _rope_apply(x, cos, sin_signed, first_half):
    width = x.shape[1]
    partner = jnp.where(first_half, pltpu.roll(x, width - ROT_HALF, 1), pltpu.roll(x, ROT_HALF, 1))
    return x * cos + partner * sin_signed


def _even_in_kernel(x_ref, g_ref, w_ref, cw_ref, qg_ref, kg_ref, cs_ref, ex_ref, exadd_ref, bdq_ref, bdk_ref,
                    ya_ref, q_ref, k_ref, v_ref, iq_ref, ik_ref, iw_ref, zbuf):
    tq = x_ref.shape[1]

    @pl.when(pl.program_id(1) == 0)
    def _():
        zbuf[0:SUBLANES, :] = jnp.zeros((SUBLANES, SC_WIDTH), F32)

    h = _rmsnorm_rows(x_ref[0], g_ref[...]).astype(BF16)

    c_q = 3 * SC_WIDTH
    c_kv = c_q + ATT_HEADS * HEAD_DIM
    c_iq = c_kv + LANES
    c_ik = c_iq + IDX_HEADS * IDX_DIM
    pq = _dot(h, w_ref[:, c_q:c_kv])
    pkv = _dot(h, w_ref[:, c_kv:c_iq])
    piq = _dot(h, w_ref[:, c_iq:c_ik])
    pik = _dot(h, w_ref[:, c_ik:c_ik + LANES])
    p_sc = _dot(h, w_ref[:, 0:c_q])
    tab = _dot_hilo(cs_ref[0], ex_ref[...]) + exadd_ref[...]
    ms = _dot_hilo(pq * pq, bdq_ref[...])
    ms_k = _dot_hilo(pkv * pkv, bdk_ref[...])

    bg = p_sc[:, 0:SC_WIDTH]
    z = p_sc[:, SC_WIDTH:2 * SC_WIDTH] * p_sc[:, 2 * SC_WIDTH:3 * SC_WIDTH]
    zbuf[SUBLANES:SUBLANES + tq, :] = z
    cw = cw_ref[...]
    conv = (cw[2:3, :] * z + cw[1:2, :] * zbuf[SUBLANES - 1:SUBLANES - 1 + tq, :]
            + cw[0:1, :] * zbuf[SUBLANES - 2:SUBLANES - 2 + tq, :])
    ya_ref[0] = (bg * conv).astype(BF16)
    zbuf[0:SUBLANES, :] = zbuf[tq:tq + SUBLANES, :]

    cos_p, sin_p = tab[:, 0:LANES], tab[:, LANES:2 * LANES]
    cos_k, sin_k = tab[:, 2 * LANES:3 * LANES], tab[:, 3 * LANES:4 * LANES]
    lane = lax.broadcasted_iota(jnp.int32, (tq, LANES), 1)
    first_half = (lane % HEAD_DIM) < ROT_HALF

    qn = pq * lax.rsqrt(ms + EPS) * qg_ref[...]
    qr = _rope_apply(qn, jnp.concatenate([cos_p] * 4, axis=1), jnp.concatenate([sin_p] * 4, axis=1),
                     jnp.concatenate([first_half] * 4, axis=1)) * (ATT_SCALE * LOG2E)
    qt = qr.T.astype(BF16)
    pad = jnp.zeros((LANES - HEAD_DIM, tq), BF16)
    for hh in range(ATT_HEADS):
        q_ref[0, hh, 0:HEAD_DIM, :] = qt[hh * HEAD_DIM:(hh + 1) * HEAD_DIM, :]
        q_ref[0, hh, HEAD_DIM:LANES, :] = pad

    is_key = lane < HEAD_DIM
    kvn = pkv * jnp.where(is_key, lax.rsqrt(ms_k + EPS) * kg_ref[...], 1.0)
    kvr = _rope_apply(kvn, cos_k, sin_k, first_half)
    k_ref[0] = jnp.where(is_key, kvr, jnp.where(lane == HEAD_DIM, 1.0, 0.0)).astype(BF16)
    vt = jnp.concatenate([kvr.T[HEAD_DIM:LANES, :], jnp.ones((LANES - HEAD_DIM, tq), F32)], axis=0).astype(BF16)
    for cc in range(tq // ATT_TILE):
        v_ref[0, cc] = vt[:, cc * ATT_TILE:(cc + 1) * ATT_TILE]

    iqr = _rope_apply(piq, jnp.concatenate([cos_p] * 2, axis=1), jnp.concatenate([sin_p] * 2, axis=1),
                      jnp.concatenate([first_half] * 2, axis=1))
    iqt = iqr.T.astype(BF16)
    for hh in range(IDX_HEADS):
        iq_ref[0, hh] = iqt[hh * IDX_DIM:(hh + 1) * IDX_DIM, :]

    ikr = _rope_apply(pik, cos_k, sin_k, first_half)
    ik_ref[0] = ikr[:, 0:IDX_DIM].astype(BF16)
    iw_ref[0] = ikr.T[IDX_DIM:IDX_DIM + IDX_HEADS, :] * IDX_SCALE


def _even_in_proj(x, cs, norm_g, w_in, conv_w, q_g, k_g):
    bsz, s_len, _ = x.shape
    tq = min(TOK_TILE, s_len)
    w = jnp.pad(w_in, ((0, 0), (0, EVEN_PAD - EVEN_COLS))).astype(BF16)
    expand, add = _rope_expand_consts()
    hd = np.arange(ATT_HEADS * HEAD_DIM) // HEAD_DIM
    bdq = jnp.asarray((hd[:, None] == hd[None, :]) / HEAD_DIM, BF16)
    kk = np.arange(LANES)
    bdk = jnp.asarray(((kk[:, None] < HEAD_DIM) & (kk[None, :] < HEAD_DIM)) / HEAD_DIM, BF16)
    consts = [
        norm_g.reshape(1, D_MODEL), w, conv_w,
        jnp.tile(q_g, ATT_HEADS).reshape(1, -1),
        jnp.concatenate([k_g, jnp.ones((LANES - HEAD_DIM,), F32)]).reshape(1, LANES),
    ]
    consts2 = [jnp.asarray(expand, BF16), jnp.asarray(add, F32), bdq, bdk]
    tok = lambda width: pl.BlockSpec((1, tq, width), lambda b, j: (b, j, 0))
    per_tile = tq // ATT_TILE
    heads_t = lambda n, d: pl.BlockSpec((1, n, d, tq), lambda b, j: (b, 0, 0, j))
    return pl.pallas_call(
        _even_in_kernel,
        grid=(bsz, s_len // tq),
        in_specs=[tok(D_MODEL)] + [_const_spec(c) for c in consts] + [tok(2 * ROT_HALF)]
                 + [_const_spec(c) for c in consts2],
        out_specs=[tok(SC_WIDTH), heads_t(ATT_HEADS, LANES), tok(LANES),
                   pl.BlockSpec((1, per_tile, LANES, ATT_TILE), lambda b, j: (b, j, 0, 0)),
                   heads_t(IDX_HEADS, IDX_DIM), tok(IDX_DIM),
                   pl.BlockSpec((1, IDX_HEADS, tq), lambda b, j: (b, 0, j))],
        out_shape=[
            jax.ShapeDtypeStruct((bsz, s_len, SC_WIDTH), BF16),
            jax.ShapeDtypeStruct((bsz, ATT_HEADS, LANES, s_len), BF16),
            jax.ShapeDtypeStruct((bsz, s_len, LANES), BF16),
            jax.ShapeDtypeStruct((bsz, s_len // ATT_TILE, LANES, ATT_TILE), BF16),
            jax.ShapeDtypeStruct((bsz, IDX_HEADS, IDX_DIM, s_len), BF16),
            jax.ShapeDtypeStruct((bsz, s_len, IDX_DIM), BF16),
            jax.ShapeDtypeStruct((bsz, IDX_HEADS, s_len), F32),
        ],
        scratch_shapes=[pltpu.VMEM((tq + SUBLANES, SC_WIDTH), F32)],
        compiler_params=pltpu.CompilerParams(dimension_semantics=("arbitrary", "arbitrary"),
                                             vmem_limit_bytes=VMEM_LIMIT),
        name="even_in_proj",
    )(x, *consts, cs, *consts2)


def _group_reduce(x, op, group=SUBLANES):
    parts = [x[r:r + group, :] for r in range(0, x.shape[0], group)]
    while len(parts) > 1:
        nxt = [op(parts[a], parts[a + 1]) for a in range(0, len(parts) - 1, 2)]
        if len(parts) % 2:
            nxt.append(parts[-1])
        parts = nxt
    return parts[0]


def _rep8(x8, op):
    return jnp.broadcast_to(op(x8, axis=0, keepdims=True), x8.shape)


def _ceil_bf16(x):
    bits = lax.bitcast_convert_type(x, jnp.int32)
    bits = jnp.where(bits >= 0, bits + 0xFFFF, bits)
    return lax.bitcast_convert_type(bits & jnp.int32(-65536), F32).astype(BF16)


def _dsa_kernel(iq_ref, iw_ref, q_ref, ik_ref, k_ref, v_ref, tri_ref, out_ref,
                sc_ref, scb_ref, iqa_ref, qa_ref, acc_ref, lg_a, lg_b, *, topk):
    tq = ATT_TILE
    ck = ATT_TILE
    i = pl.program_id(1)
    nk = i + 1
    qpos = i * tq + lax.broadcasted_iota(jnp.int32, (SUBLANES, tq), 1)
    keff = jnp.minimum(qpos + 1, topk).astype(F32)
    iw = iw_ref[0]
    inf8 = jnp.full((SUBLANES, tq), jnp.inf, F32)

    for hh in range(IDX_HEADS):
        iqa_ref[:, hh * tq:(hh + 1) * tq] = iq_ref[0, hh]

    def scores(c):
        off = pl.multiple_of(c * ck, ck)
        z = _dot(ik_ref[0, pl.ds(off, ck), :], iqa_ref[...])
        acc = jnp.zeros((ck, tq), F32)
        for hh in range(IDX_HEADS):
            acc = acc + jnp.maximum(z[:, hh * tq:(hh + 1) * tq], 0.0) * iw[hh:hh + 1, :]
        return acc

    def score_chunk(c, carry):
        lo, hi = carry
        s = scores(c)
        sc_ref[c] = s
        scb_ref[c] = _ceil_bf16(s)
        return jnp.minimum(lo, _group_reduce(s, jnp.minimum)), jnp.maximum(hi, _group_reduce(s, jnp.maximum))

    lo, hi = lax.fori_loop(0, i // 2, lambda t, cr: score_chunk(2 * t + 1, score_chunk(2 * t, cr)), (inf8, -inf8))
    lo, hi = lax.fori_loop(0, i % 2, lambda _, cr: score_chunk(i - 1, cr), (lo, hi))
    s = scores(i)
    kpos = i * ck + lax.broadcasted_iota(jnp.int32, (ck, tq), 0)
    valid = kpos <= i * tq + lax.broadcasted_iota(jnp.int32, (ck, tq), 1)
    sc_ref[i] = jnp.where(valid, s, -jnp.inf)
    scb_ref[i] = _ceil_bf16(jnp.where(valid, s, -jnp.inf))
    lo = _rep8(jnp.minimum(lo, _group_reduce(jnp.where(valid, s, jnp.inf), jnp.minimum)), jnp.min)
    hi = _rep8(jnp.maximum(hi, _group_reduce(jnp.where(valid, s, -jnp.inf), jnp.maximum)), jnp.max)

    def pivot(lo, hi):
        mid = 0.5 * lo + 0.5 * hi
        return jnp.where(mid < hi, jnp.maximum(mid, lo), lo)

    def coarse_body(_, carry):
        lo, hi, chi = carry
        v = pivot(lo, hi).astype(BF16).astype(F32)
        vb = jnp.concatenate([v, v], axis=0).astype(BF16)
        one = jnp.ones((2 * SUBLANES, tq), BF16)
        zero = jnp.zeros((2 * SUBLANES, tq), BF16)

        def body(c, accs):
            accs = list(accs)
            for r in range(ck // (2 * SUBLANES)):
                x = scb_ref[c, r * 2 * SUBLANES:(r + 1) * 2 * SUBLANES, :]
                accs[r % 4] = accs[r % 4] + jnp.where(x > vb, one, zero)
            return tuple(accs)

        a0, a1, a2, a3 = lax.fori_loop(0, nk, body, (zero, zero, zero, zero))
        cnt = jnp.sum(((a0 + a1) + (a2 + a3)).astype(F32), axis=0, keepdims=True)
        inside = jnp.logical_and(v >= lo, v < hi)
        up = jnp.logical_and(inside, cnt >= keff)
        down = jnp.logical_and(inside, cnt < keff)
        return jnp.where(up, v, lo), jnp.where(down, v, hi), jnp.where(down, cnt, chi)

    lo, hi, chi = lax.fori_loop(0, COARSE_STEPS, coarse_body, (lo, hi, jnp.zeros((SUBLANES, tq), F32)))

    def count_body(_, carry):
        lo, hi, chi = carry
        v = pivot(lo, hi)

        def body(c, accs):
            accs = list(accs)
            for r in range(ck // SUBLANES):
                x = sc_ref[c, r * SUBLANES:(r + 1) * SUBLANES, :]
                accs[r % 4] = accs[r % 4] + jnp.where(x > v, 1.0, 0.0)
            return tuple(accs)

        zero = jnp.zeros((SUBLANES, tq), F32)
        a0, a1, a2, a3 = lax.fori_loop(0, nk, body, (zero, zero, zero, zero))
        cnt = _rep8((a0 + a1) + (a2 + a3), jnp.sum)
        ge = cnt >= keff
        return jnp.where(ge, v, lo), jnp.where(ge, hi, v), jnp.where(ge, chi, cnt)

    lo, hi, chi = lax.fori_loop(0, COUNT_STEPS, count_body, (lo, hi, chi))

    def tight_cond(state):
        it, pending = state[0], state[1]
        return jnp.logical_and(pending > 0, it < MAX_TIGHTEN_STEPS)

    def tight_body(state):
        it, _, lo, hi, chi = state
        v = pivot(lo, hi)

        def body(c, accs):
            cnt, below, above = accs
            for r in range(ck // SUBLANES):
                x = sc_ref[c, r * SUBLANES:(r + 1) * SUBLANES, :]
                gt = x > v
                cnt = cnt + jnp.where(gt, 1.0, 0.0)
                below = jnp.maximum(below, jnp.where(gt, -jnp.inf, x))
                above = jnp.minimum(above, jnp.where(gt, x, jnp.inf))
            return cnt, below, above

        cnt, below, above = lax.fori_loop(0, nk, body, (jnp.zeros((SUBLANES, tq), F32), -inf8, inf8))
        cnt = _rep8(cnt, jnp.sum)
        below = _rep8(below, jnp.max)
        above = _rep8(above, jnp.min)
        ge = cnt >= keff
        lo = jnp.where(ge, above, lo)
        chi = jnp.where(ge, chi, cnt)
        hi = jnp.where(ge, hi, below)
        pending = jnp.max(jnp.where(lo < hi, 1.0, 0.0)) > 0.5
        return it + 1, pending.astype(jnp.int32), lo, hi, chi

    state = lax.while_loop(tight_cond, tight_body, (jnp.int32(0), jnp.int32(1), lo, hi, chi))
    thr = state[3][0:1, :]
    ties_wanted = (keff - state[4])[0:1, :]

    for hh in range(ATT_HEADS):
        qa_ref[:, hh * tq:(hh + 1) * tq] = q_ref[0, hh]
    acc_ref[...] = jnp.zeros(acc_ref.shape, F32)

    def logits(c, buf, ties_seen):
        x = sc_ref[c]
        eq = x == thr
        prefix = _dot(tri_ref[...], jnp.where(eq, 1.0, 0.0).astype(BF16))
        sel = jnp.logical_or(x > thr, jnp.logical_and(eq, prefix <= ties_wanted - ties_seen))
        bias = jnp.where(sel, 0.0, NEG_BIG).astype(BF16)
        off = pl.multiple_of(c * ck, ck)
        sb = _dot(k_ref[0, pl.ds(off, ck), :], qa_ref[...]).astype(BF16)
        buf[...] = sb + jnp.concatenate([bias] * ATT_HEADS, axis=1)
        return ties_seen + prefix[ck - 1:ck, :]

    def accumulate(c, buf, m_old):
        sb = buf[...]
        cm = jnp.max(_group_reduce(sb, jnp.maximum, 2 * SUBLANES).astype(F32), axis=0, keepdims=True)
        m_new = jnp.maximum(m_old, cm)
        p = jnp.exp2(sb - m_new.astype(BF16))
        acc_ref[...] = acc_ref[...] * jnp.exp2(m_old - m_new) + _dot(v_ref[0, c], p)
        return m_new

    def att_pair(t, carry):
        m, ties = carry
        ties = logits(2 * t + 1, lg_b, ties)
        m = accumulate(2 * t, lg_a, m)
        ties = logits(2 * t + 2, lg_a, ties)
        return accumulate(2 * t + 1, lg_b, m), ties

    def att_tail(_, carry):
        m, ties = carry
        ties = logits(i, lg_b, ties)
        return accumulate(i, lg_b, m), ties

    ties = logits(0, lg_a, jnp.zeros((1, tq), F32))
    m_run = jnp.full((1, ATT_HEADS * tq), NEG_BIG, BF16).astype(F32)
    m_run, ties = lax.fori_loop(0, i // 2, att_pair, (m_run, ties))
    m_run = accumulate(2 * (i // 2), lg_a, m_run)
    lax.fori_loop(0, i % 2, att_tail, (m_run, ties))
    outs = []
    for hh in range(ATT_HEADS):
        a = acc_ref[:, hh * tq:(hh + 1) * tq]
        outs.append(a[0:HEAD_DIM, :] * (1.0 / a[HEAD_DIM:HEAD_DIM + 1, :]))
    out_ref[0] = jnp.concatenate(outs, axis=0).T.astype(BF16)


def _sparse_attention(qt, k_aug, vt_ext, iqt, ik, iwt):
    bsz, _, _, s_len = qt.shape
    tq = ATT_TILE
    topk = min(TOPK_MAX, s_len // 4)
    n_chunks = s_len // tq
    idx = np.arange(tq)
    tri = jnp.asarray(idx[None, :] <= idx[:, None], BF16)
    assert n_chunks * tq // (2 * SUBLANES) <= 256
    heads_t = lambda n, d: pl.BlockSpec((1, n, d, tq), lambda b, j: (b, 0, 0, j))
    return pl.pallas_call(
        functools.partial(_dsa_kernel, topk=topk),
        grid=(bsz, n_chunks),
        in_specs=[heads_t(IDX_HEADS, IDX_DIM), pl.BlockSpec((1, IDX_HEADS, tq), lambda b, j: (b, 0, j)),
                  heads_t(ATT_HEADS, LANES),
                  pl.BlockSpec((1, s_len, IDX_DIM), lambda b, j: (b, 0, 0)),
                  pl.BlockSpec((1, s_len, LANES), lambda b, j: (b, 0, 0)),
                  pl.BlockSpec((1, n_chunks, LANES, tq), lambda b, j: (b, 0, 0, 0)),
                  _const_spec(tri)],
        out_specs=pl.BlockSpec((1, tq, ATT_HEADS * HEAD_DIM), lambda b, j: (b, j, 0)),
        out_shape=jax.ShapeDtypeStruct((bsz, s_len, ATT_HEADS * HEAD_DIM), BF16),
        scratch_shapes=[
            pltpu.VMEM((n_chunks, tq, tq), F32),
            pltpu.VMEM((n_chunks, tq, tq), BF16),
            pltpu.VMEM((IDX_DIM, IDX_HEADS * tq), BF16),
            pltpu.VMEM((LANES, ATT_HEADS * tq), BF16),
            pltpu.VMEM((LANES, ATT_HEADS * tq), F32),
            pltpu.VMEM((tq, ATT_HEADS * tq), BF16),
            pltpu.VMEM((tq, ATT_HEADS * tq), BF16),
        ],
        compiler_params=pltpu.CompilerParams(dimension_semantics=("arbitrary", "arbitrary"),
                                             vmem_limit_bytes=VMEM_LIMIT),
        name="sparse_attention",
    )(iqt, iwt, qt, ik, k_aug, vt_ext, tri)


def _mix_ffn_kernel(x_ref, ya_ref, yb_ref, wo_ref, fg_ref, wup_ref, wd_ref, cw_ref, cb_ref,
                    out_ref, h_ref, act_ref, carry_ref):
    tq = x_ref.shape[1]
    half = ya_ref.shape[2]

    @pl.when(pl.program_id(1) == 0)
    def _():
        carry_ref[...] = jnp.zeros(carry_ref.shape, F32)

    x1 = x_ref[0] + _dot(ya_ref[0], wo_ref[0:half, :]) + _dot(yb_ref[0], wo_ref[half:2 * half, :])
    out_ref[0] = x1
    h_ref[...] = _rmsnorm_rows(x1, fg_ref[...]).astype(BF16)
    first_row = lax.broadcasted_iota(jnp.int32, (SUBLANES, FF_CHUNK), 0) == 0

    def shift_down(a, slot):
        prev = carry_ref[slot]
        carry_ref[slot] = a[tq - SUBLANES:tq, :]
        r = pltpu.roll(a, 1, 0)
        head = jnp.where(first_row, pltpu.roll(prev, 1, 0), r[0:SUBLANES, :])
        return jnp.concatenate([head, r[SUBLANES:, :]], axis=0)

    def conv(cols, slot):
        u = _dot(h_ref[...], wup_ref[:, cols])
        cw = cw_ref[:, cols]
        inner = cw[1:2, :] * u + shift_down(cw[0:1, :] * u, 2 * slot)
        return cw[2:3, :] * u + shift_down(inner, 2 * slot + 1) + cb_ref[:, cols]

    for c in range(D_FF // FF_CHUNK):
        gate = conv(slice(c * FF_CHUNK, (c + 1) * FF_CHUNK), 2 * c)
        up = conv(slice(D_FF + c * FF_CHUNK, D_FF + (c + 1) * FF_CHUNK), 2 * c + 1)
        act_ref[:, c * FF_CHUNK:(c + 1) * FF_CHUNK] = (gate * jax.nn.sigmoid(gate) * up).astype(BF16)
    out_ref[0] += _dot(act_ref[...], wd_ref[...])


def _mix_ffn(x, ya, yb, w_out, ffn_g, w_up, conv_w, conv_b, w_down):
    bsz, s_len, _ = x.shape
    tq = min(FFN_TILE, s_len)
    half = ya.shape[2]
    n_ff = D_FF // FF_CHUNK
    consts = [w_out.astype(BF16), ffn_g.reshape(1, D_MODEL), w_up.astype(BF16), w_down.astype(BF16),
              conv_w, conv_b.reshape(1, -1)]
    tok = lambda width: pl.BlockSpec((1, tq, width), lambda b, j: (b, j, 0))
    return pl.pallas_call(
        _mix_ffn_kernel,
        grid=(bsz, s_len // tq),
        in_specs=[tok(D_MODEL), tok(half), tok(half)] + [_const_spec(c) for c in consts],
        out_specs=tok(D_MODEL),
        out_shape=jax.ShapeDtypeStruct((bsz, s_len, D_MODEL), F32),
        scratch_shapes=[
            pltpu.VMEM((tq, D_MODEL), BF16),
            pltpu.VMEM((tq, D_FF), BF16),
            pltpu.VMEM((4 * n_ff, SUBLANES, FF_CHUNK), F32),
        ],
        compiler_params=pltpu.CompilerParams(dimension_semantics=("arbitrary", "arbitrary"),
                                             vmem_limit_bytes=VMEM_LIMIT),
        name="mix_out_ffn",
    )(x, ya, yb, *consts)


def _odd_in_kernel(x_ref, g_ref, w_ref, lng_ref, lnb_ref, sgw_ref, sgb_ref, wgate_ref, bgate_ref, onorm_ref,
                   tril_ref, yc_ref, yd_ref, state_ref, obuf):
    tq = x_ref.shape[1]
    hk = GLA_HEADS * GLA_DK

    @pl.when(pl.program_id(1) == 0)
    def _():
        state_ref[...] = jnp.zeros(state_ref.shape, F32)

    h = _rmsnorm_rows(x_ref[0], g_ref[...]).astype(BF16)

    c_q = 2 * SG_WIDTH
    c_v = c_q + 2 * hk
    c_r = c_v + GLA_HEADS * GLA_DV
    c_g = c_r + GLA_HEADS * GLA_DV
    pg = _dot(h, w_ref[:, c_g:c_g + LANES])
    qf = _dot(h, w_ref[:, c_q:c_q + hk]) * (GLA_DK ** -0.5)
    kf = _dot(h, w_ref[:, c_q + hk:c_v])
    vf = _dot(h, w_ref[:, c_v:c_r]).astype(BF16)
    rr = _dot(h, w_ref[:, c_r:c_g])
    uv = _dot(h, w_ref[:, 0:c_q])
    g = jax.nn.log_sigmoid(_dot(pg.astype(BF16), wgate_ref[...]) + bgate_ref[...]) * (1.0 / GATE_NORMALIZER)
    g_hi, g_lo = _split_bf16(g)
    b_parts, tot_parts = [], []
    for c in range(tq // GLA_CHUNK):
        rows = slice(c * GLA_CHUNK, (c + 1) * GLA_CHUNK)
        bc = _dot(tril_ref[...], g_hi[rows, :]) + _dot(tril_ref[...], g_lo[rows, :])
        b_parts.append(bc)
        tot_parts.append(jnp.broadcast_to(bc[GLA_CHUNK - 1:GLA_CHUNK, :], bc.shape))
    b = jnp.concatenate(b_parts, axis=0)
    gtot = jnp.concatenate(tot_parts, axis=0)
    qp = qf * jnp.exp(b)
    kp = (kf * jnp.exp(-b)).astype(BF16)
    kpp = kf * jnp.exp(gtot - b)
    dec = jnp.exp(gtot)

    gel = 0.5 * uv * (1.0 + lax.erf(uv * np.float32(np.sqrt(0.5))))
    u = gel[:, 0:SG_WIDTH]
    vv = gel[:, SG_WIDTH:2 * SG_WIDTH]
    mu = jnp.mean(vv, axis=-1, keepdims=True)
    xc = vv - mu
    vn = (xc * lax.rsqrt(jnp.mean(xc * xc, axis=-1, keepdims=True) + LN_EPS) * lng_ref[...] + lnb_ref[...])
    vn = vn.astype(BF16)
    low = lax.broadcasted_iota(jnp.int32, (SG_CHUNK, LANES), 1) < SG_GROUP_DIM
    zero = jnp.zeros((SG_CHUNK, LANES), BF16)
    for cc in range(tq // SG_CHUNK):
        rows = slice(cc * SG_CHUNK, (cc + 1) * SG_CHUNK)
        for p in range(SG_WIDTH // LANES):
            cols = slice(p * LANES, (p + 1) * LANES)
            tile = vn[rows, cols]
            rhs = jnp.concatenate([jnp.where(low, tile, zero), jnp.where(low, zero, tile)], axis=0)
            sp = _dot(sgw_ref[p], rhs) + sgb_ref[:, cols]
            yc_ref[0, rows, cols] = (u[rows, cols] * sp).astype(BF16)

    lane_head = lax.broadcasted_iota(jnp.int32, (GLA_CHUNK, hk), 1) // GLA_DK
    rr_i = lax.broadcasted_iota(jnp.int32, (GLA_HEADS * GLA_CHUNK, GLA_CHUNK), 0) % GLA_CHUNK
    cc_i = lax.broadcasted_iota(jnp.int32, (GLA_HEADS * GLA_CHUNK, GLA_CHUNK), 1)
    causal = rr_i >= cc_i
    chunk_rows = [slice(c * GLA_CHUNK, (c + 1) * GLA_CHUNK) for c in range(tq // GLA_CHUNK)]
    head_rows = [slice(hh * GLA_DK, (hh + 1) * GLA_DK) for hh in range(GLA_HEADS)]
    head_cols = [slice(hh * GLA_DV, (hh + 1) * GLA_DV) for hh in range(GLA_HEADS)]
    lhs_all, a_all, kv_all, dcol_all = [], [], [], []
    for rows in chunk_rows:
        qc = qp[rows, :]
        lhs = jnp.concatenate([jnp.where(lane_head == hh, qc, 0.0) for hh in range(GLA_HEADS)], axis=0).astype(BF16)
        lhs_all.append(lhs)
        a_all.append(jnp.where(causal, _dot_nt(lhs, kp[rows, :]), 0.0).astype(BF16))
    for rows in chunk_rows:
        kt = kpp[rows, :].T.astype(BF16)
        kv_all.append(jnp.concatenate([_dot(kt[hr, :], vf[rows, hc]) for hr, hc in zip(head_rows, head_cols)], axis=0))
        dcol_all.append(dec[rows, :].T[:, 0:1])
    states = [state_ref[...]]
    for kvd, dcol in zip(kv_all, dcol_all):
        states.append(states[-1] * dcol + kvd)
    state_ref[...] = states[-1]
    for rows, lhs, a, state in zip(chunk_rows, lhs_all, a_all, states):
        o_inter = _dot(lhs, state.astype(BF16))
        for hr, hc in zip(head_rows, head_cols):
            obuf[rows, hc] = o_inter[hr, :] + _dot(a[hr, :], vf[rows, hc])

    for hh in range(GLA_HEADS):
        hc = slice(hh * GLA_DV, (hh + 1) * GLA_DV)
        o = _rmsnorm_rows(obuf[:, hc], onorm_ref[...])
        r = rr[:, hc]
        yd_ref[0, :, hc] = (o * (r * jax.nn.sigmoid(r))).astype(BF16)


def _odd_in_proj(x, norm_g, w_in, ln_g, ln_b, sg_w, sg_b, w_gate, b_gate, o_norm):
    bsz, s_len, _ = x.shape
    tq = min(TOK_TILE, s_len)
    hk = GLA_HEADS * GLA_DK
    c_glr = 2 * SG_WIDTH + 2 * hk + GLA_HEADS * GLA_DV
    w = jnp.concatenate([w_in[:, :c_glr], w_in[:, c_glr + GLA_RANK:], w_in[:, c_glr:c_glr + GLA_RANK]], axis=1)
    w = jnp.pad(w, ((0, 0), (0, ODD_PAD - w.shape[1]))).astype(BF16)
    wm = jnp.tril(sg_w)
    sgw = wm.reshape(SG_GROUPS // 2, 2, SG_CHUNK, SG_CHUNK).transpose(0, 2, 1, 3)
    sgw = sgw.reshape(SG_GROUPS // 2, SG_CHUNK, 2 * SG_CHUNK).astype(BF16)
    sgb = jnp.repeat(sg_b.T, SG_GROUP_DIM, axis=1)
    wgate = jnp.pad(w_gate, ((0, LANES - GLA_RANK), (0, 0))).astype(BF16)
    idx = np.arange(GLA_CHUNK)
    tril = jnp.asarray(idx[None, :] <= idx[:, None], BF16)
    consts = [norm_g.reshape(1, D_MODEL), w, ln_g.reshape(1, -1), ln_b.reshape(1, -1), sgw, sgb, wgate,
              b_gate.reshape(1, -1), o_norm.reshape(1, -1), tril]
    tok = lambda width: pl.BlockSpec((1, tq, width), lambda b, j: (b, j, 0))
    return pl.pallas_call(
        _odd_in_kernel,
        grid=(bsz, s_len // tq),
        in_specs=[tok(D_MODEL)] + [_const_spec(c) for c in consts],
        out_specs=[tok(SG_WIDTH), tok(GLA_HEADS * GLA_DV)],
        out_shape=[jax.ShapeDtypeStruct((bsz, s_len, SG_WIDTH), BF16),
                   jax.ShapeDtypeStruct((bsz, s_len, GLA_HEADS * GLA_DV), BF16)],
        scratch_shapes=[pltpu.VMEM((hk, GLA_DV), F32), pltpu.VMEM((tq, GLA_HEADS * GLA_DV), F32)],
        compiler_params=pltpu.CompilerParams(dimension_semantics=("arbitrary", "arbitrary"),
                                             vmem_limit_bytes=VMEM_LIMIT),
        name="odd_in_proj",
    )(x, *consts)


def kernel(x, positions, ev_norm, ev_w_in, sc_conv_w, q_norm, k_norm, ev_w_out, od_norm, od_w_in, sg_ln_g, sg_ln_b,
           sg_w, sg_b, gla_w_gate, gla_b_gate, gla_o_norm, od_w_out, ffn_norm, ffn_w_up, ffn_conv_w, ffn_conv_b,
           ffn_w_down):
    depth = ffn_norm.shape[0]
    cs = _rope_tables(positions)
    for layer in range(depth):
        i = layer // 2
        if layer % 2 == 0:
            ya, q, k, v, iq, ik, iw = _even_in_proj(x, cs, ev_norm[i], ev_w_in[i], sc_conv_w[i], q_norm[i], k_norm[i])
            yb = _sparse_attention(q, k, v, iq, ik, iw)
            w_out = ev_w_out[i]
        else:
            ya, yb = _odd_in_proj(x, od_norm[i], od_w_in[i], sg_ln_g[i], sg_ln_b[i], sg_w[i], sg_b[i],
                                  gla_w_gate[i], gla_b_gate[i], gla_o_norm[i])
            w_out = od_w_out[i]
        x = _mix_ffn(x, ya, yb, w_out, ffn_norm[layer], ffn_w_up[layer], ffn_conv_w[layer], ffn_conv_b[layer],
                     ffn_w_down[layer])
    return x
```

```python
import functools

import numpy as np
import jax
import jax.numpy as jnp
from jax import lax
from jax.experimental import pallas as pl
from jax.experimental.pallas import tpu as pltpu

F32 = jnp.float32
BF16 = jnp.bfloat16

D_MODEL = 1024
SC_WIDTH = 512
CONV_WIDTH = 3
ATT_HEADS = 8
HEAD_DIM = 64
IDX_HEADS = 4
IDX_DIM = 64
TOPK_MAX = 256
ROT_DIM = HEAD_DIM // 4
ROT_HALF = ROT_DIM // 2
ROPE_THETA = 500000.0
IDX_SCALE = (IDX_HEADS * IDX_DIM) ** -0.5
ATT_SCALE = HEAD_DIM ** -0.5
SG_GROUPS = 8
SG_GROUP_DIM = 64
SG_WIDTH = 512
SG_CHUNK = 128
GLA_HEADS = 4
GLA_DK = 64
GLA_DV = 128
GLA_RANK = 16
GLA_CHUNK = 64
GATE_NORMALIZER = 16.0
D_FF = 2816
EPS = 1e-6
LN_EPS = 1e-5

EVEN_COLS = 2500
EVEN_PAD = 2560
ODD_PAD = 2688

LANES = 128
SUBLANES = 8
VMEM_LIMIT = 56 * 1024 * 1024

TOK_TILE = 512
FFN_TILE = 1024
ATT_TILE = 256
COARSE_STEPS = 12
COUNT_STEPS = 2
MAX_TIGHTEN_STEPS = 512
FF_CHUNK = 256
PROLOGUE_ROWS = 256
NEG_BIG = -1e30
LOG2E = 1.4426950408889634


def _const_spec(arr):
    nd = arr.ndim
    return pl.BlockSpec(arr.shape, lambda *_: (0,) * nd, pipeline_mode=pl.Buffered(1))


def _dot(a, b):
    return jnp.dot(a, b, preferred_element_type=F32)


def _dot_nt(a, b):
    return lax.dot_general(a, b, (((1,), (1,)), ((), ())), preferred_element_type=F32)


def _split_bf16(a):
    hi = a.astype(BF16)
    lo = (a - hi.astype(F32)).astype(BF16)
    return hi, lo


def _dot_hilo(a, b):
    hi, lo = _split_bf16(a)
    return _dot(hi, b) + _dot(lo, b)


def _dot_hilo_left(b, a):
    hi, lo = _split_bf16(a)
    return _dot(b, hi) + _dot(b, lo)


def _rmsnorm_rows(x, g):
    return x * lax.rsqrt(jnp.mean(x * x, axis=-1, keepdims=True) + EPS) * g


def _rope_kernel(invf_ref, pos_ref, cos_ref, sin_ref):
    pos = pos_ref[...]
    for f in range(ROT_HALF):
        ang = pos * invf_ref[f]
        cos_ref[f] = jnp.cos(ang)
        sin_ref[f] = jnp.sin(ang)


def _rope_tables(positions):
    bsz, s_len = positions.shape
    inv_freq = ROPE_THETA ** (-jnp.arange(0, ROT_DIM, 2, dtype=F32) / ROT_DIM)
    cos, sin = pl.pallas_call(
        _rope_kernel,
        out_shape=[jax.ShapeDtypeStruct((ROT_HALF, bsz, s_len), F32)] * 2,
        in_specs=[pl.BlockSpec(memory_space=pltpu.SMEM), pl.BlockSpec(memory_space=pltpu.VMEM)],
        out_specs=[pl.BlockSpec(memory_space=pltpu.VMEM)] * 2,
        name="rope_tables",
    )(inv_freq, positions.astype(F32))
    return cos.transpose(1, 0, 2), sin.transpose(1, 0, 2)


def _rope_rows(xt, cos, sin):
    parts = []
    for r0 in range(0, xt.shape[0], HEAD_DIM):
        x1 = xt[r0:r0 + ROT_HALF, :]
        x2 = xt[r0 + ROT_HALF:r0 + ROT_DIM, :]
        parts += [x1 * cos - x2 * sin, x1 * sin + x2 * cos, xt[r0 + ROT_DIM:r0 + HEAD_DIM, :]]
    return jnp.concatenate(parts, axis=0)


def _rms_norm_rows_t(xt, gain_col):
    parts = []
    for r0 in range(0, xt.shape[0], HEAD_DIM):
        xh = xt[r0:r0 + HEAD_DIM, :]
        ms = jnp.sum(xh * xh, axis=0, keepdims=True) * (1.0 / HEAD_DIM)
        parts.append(xh * lax.rsqrt(ms + EPS) * gain_col)
    return jnp.concatenate(parts, axis=0)


def _even_in_kernel(x_ref, g_ref, w_ref, cw_ref, qg_ref, kg_ref, cos_ref, sin_ref,
                    ya_ref, q_ref, k_ref, v_ref, iq_ref, ik_ref, iw_ref, zbuf):
    tq = x_ref.shape[1]

    @pl.when(pl.program_id(1) == 0)
    def _():
        zbuf[0:SUBLANES, :] = jnp.zeros((SUBLANES, SC_WIDTH), F32)

    h = _rmsnorm_rows(x_ref[0], g_ref[...]).astype(BF16)

    c_q = 3 * SC_WIDTH
    c_kv = c_q + ATT_HEADS * HEAD_DIM
    c_iq = c_kv + LANES
    c_ik = c_iq + IDX_HEADS * IDX_DIM
    pq = _dot(h, w_ref[:, c_q:c_kv])
    pkv = _dot(h, w_ref[:, c_kv:c_iq])
    piq = _dot(h, w_ref[:, c_iq:c_ik])
    pik = _dot(h, w_ref[:, c_ik:c_ik + LANES])
    p_sc = _dot(h, w_ref[:, 0:c_q])

    bg = p_sc[:, 0:SC_WIDTH]
    z = p_sc[:, SC_WIDTH:2 * SC_WIDTH] * p_sc[:, 2 * SC_WIDTH:3 * SC_WIDTH]
    zbuf[SUBLANES:SUBLANES + tq, :] = z
    cw = cw_ref[...]
    conv = (cw[2:3, :] * z + cw[1:2, :] * zbuf[SUBLANES - 1:SUBLANES - 1 + tq, :]
            + cw[0:1, :] * zbuf[SUBLANES - 2:SUBLANES - 2 + tq, :])
    ya_ref[0] = (bg * conv).astype(BF16)
    zbuf[0:SUBLANES, :] = zbuf[tq:tq + SUBLANES, :]

    cos = cos_ref[0]
    sin = sin_ref[0]
    gain_q = jnp.concatenate([qg_ref[...]] * (tq // LANES), axis=1)
    gain_k = jnp.concatenate([kg_ref[...]] * (tq // LANES), axis=1)

    qt = (_rope_rows(_rms_norm_rows_t(pq.T, gain_q), cos, sin) * (ATT_SCALE * LOG2E)).astype(BF16)
    pad = jnp.zeros((LANES - HEAD_DIM, tq), BF16)
    for hh in range(ATT_HEADS):
        q_ref[0, hh, 0:HEAD_DIM, :] = qt[hh * HEAD_DIM:(hh + 1) * HEAD_DIM, :]
        q_ref[0, hh, HEAD_DIM:LANES, :] = pad

    kvt = pkv.T
    kt = _rope_rows(_rms_norm_rows_t(kvt[0:HEAD_DIM, :], gain_k), cos, sin)
    k_ref[0] = jnp.concatenate([kt, jnp.zeros((LANES - HEAD_DIM, tq), F32)], axis=0).T.astype(BF16)
    vt = jnp.concatenate([kvt[HEAD_DIM:LANES, :], jnp.ones((LANES - HEAD_DIM, tq), F32)], axis=0).astype(BF16)
    for cc in range(tq // ATT_TILE):
        v_ref[0, cc] = vt[:, cc * ATT_TILE:(cc + 1) * ATT_TILE]

    iqt = _rope_rows(piq.T, cos, sin).astype(BF16)
    for hh in range(IDX_HEADS):
        iq_ref[0, hh] = iqt[hh * IDX_DIM:(hh + 1) * IDX_DIM, :]

    ikt = pik.T
    ik_rot = jnp.concatenate([_rope_rows(ikt[0:IDX_DIM, :], cos, sin), ikt[IDX_DIM:LANES, :]], axis=0)
    ik_ref[0] = ik_rot.T[:, 0:IDX_DIM].astype(BF16)
    iw_ref[0] = ikt[IDX_DIM:IDX_DIM + IDX_HEADS, :] * IDX_SCALE


def _even_in_proj(x, cos, sin, norm_g, w_in, conv_w, q_g, k_g):
    bsz, s_len, _ = x.shape
    tq = min(TOK_TILE, s_len)
    w = jnp.pad(w_in, ((0, 0), (0, EVEN_PAD - EVEN_COLS))).astype(BF16)
    as_columns = lambda g: jnp.broadcast_to(g[:, None], (HEAD_DIM, LANES))
    consts = [norm_g.reshape(1, D_MODEL), w, conv_w, as_columns(q_g), as_columns(k_g)]
    tok = lambda width: pl.BlockSpec((1, tq, width), lambda b, j: (b, j, 0))
    rope = pl.BlockSpec((1, ROT_HALF, tq), lambda b, j: (b, 0, j))
    per_tile = tq // ATT_TILE
    heads_t = lambda n, d: pl.BlockSpec((1, n, d, tq), lambda b, j: (b, 0, 0, j))
    return pl.pallas_call(
        _even_in_kernel,
        grid=(bsz, s_len // tq),
        in_specs=[tok(D_MODEL)] + [_const_spec(c) for c in consts] + [rope, rope],
        out_specs=[tok(SC_WIDTH), heads_t(ATT_HEADS, LANES), tok(LANES),
                   pl.BlockSpec((1, per_tile, LANES, ATT_TILE), lambda b, j: (b, j, 0, 0)),
                   heads_t(IDX_HEADS, IDX_DIM), tok(IDX_DIM),
                   pl.BlockSpec((1, IDX_HEADS, tq), lambda b, j: (b, 0, j))],
        out_shape=[
            jax.ShapeDtypeStruct((bsz, s_len, SC_WIDTH), BF16),
            jax.ShapeDtypeStruct((bsz, ATT_HEADS, LANES, s_len), BF16),
            jax.ShapeDtypeStruct((bsz, s_len, LANES), BF16),
            jax.ShapeDtypeStruct((bsz, s_len // ATT_TILE, LANES, ATT_TILE), BF16),
            jax.ShapeDtypeStruct((bsz, IDX_HEADS, IDX_DIM, s_len), BF16),
            jax.ShapeDtypeStruct((bsz, s_len, IDX_DIM), BF16),
            jax.ShapeDtypeStruct((bsz, IDX_HEADS, s_len), F32),
        ],
        scratch_shapes=[pltpu.VMEM((tq + SUBLANES, SC_WIDTH), F32)],
        compiler_params=pltpu.CompilerParams(dimension_semantics=("arbitrary", "arbitrary"),
                                             vmem_limit_bytes=VMEM_LIMIT),
        name="even_in_proj",
    )(x, *consts, cos, sin)


def _group_reduce(x, op, group=SUBLANES):
    parts = [x[r:r + group, :] for r in range(0, x.shape[0], group)]
    while len(parts) > 1:
        nxt = [op(parts[a], parts[a + 1]) for a in range(0, len(parts) - 1, 2)]
        if len(parts) % 2:
            nxt.append(parts[-1])
        parts = nxt
    return parts[0]


def _rep8(x8, op):
    return jnp.broadcast_to(op(x8, axis=0, keepdims=True), x8.shape)


def _ceil_bf16(x):
    bits = lax.bitcast_convert_type(x, jnp.int32)
    bits = jnp.where(bits >= 0, bits + 0xFFFF, bits)
    return lax.bitcast_convert_type(bits & jnp.int32(-65536), F32).astype(BF16)


def _dsa_kernel(iq_ref, iw_ref, q_ref, ik_ref, k_ref, v_ref, tri_ref, out_ref,
                sc_ref, scb_ref, iqa_ref, qa_ref, acc_ref, lg_a, lg_b, *, topk):
    tq = ATT_TILE
    ck = ATT_TILE
    i = pl.program_id(1)
    nk = i + 1
    qpos = i * tq + lax.broadcasted_iota(jnp.int32, (SUBLANES, tq), 1)
    keff = jnp.minimum(qpos + 1, topk).astype(F32)
    iw = iw_ref[0]
    inf8 = jnp.full((SUBLANES, tq), jnp.inf, F32)

    for hh in range(IDX_HEADS):
        iqa_ref[:, hh * tq:(hh + 1) * tq] = iq_ref[0, hh]

    def scores(c):
        off = pl.multiple_of(c * ck, ck)
        z = _dot(ik_ref[0, pl.ds(off, ck), :], iqa_ref[...])
        acc = jnp.zeros((ck, tq), F32)
        for hh in range(IDX_HEADS):
            acc = acc + jnp.maximum(z[:, hh * tq:(hh + 1) * tq], 0.0) * iw[hh:hh + 1, :]
        return acc

    def score_chunk(c, carry):
        lo, hi = carry
        s = scores(c)
        sc_ref[c] = s
        scb_ref[c] = _ceil_bf16(s)
        return jnp.minimum(lo, _group_reduce(s, jnp.minimum)), jnp.maximum(hi, _group_reduce(s, jnp.maximum))

    lo, hi = lax.fori_loop(0, i // 2, lambda t, cr: score_chunk(2 * t + 1, score_chunk(2 * t, cr)), (inf8, -inf8))
    lo, hi = lax.fori_loop(0, i % 2, lambda _, cr: score_chunk(i - 1, cr), (lo, hi))
    s = scores(i)
    kpos = i * ck + lax.broadcasted_iota(jnp.int32, (ck, tq), 0)
    valid = kpos <= i * tq + lax.broadcasted_iota(jnp.int32, (ck, tq), 1)
    sc_ref[i] = jnp.where(valid, s, -jnp.inf)
    scb_ref[i] = _ceil_bf16(jnp.where(valid, s, -jnp.inf))
    lo = _rep8(jnp.minimum(lo, _group_reduce(jnp.where(valid, s, jnp.inf), jnp.minimum)), jnp.min)
    hi = _rep8(jnp.maximum(hi, _group_reduce(jnp.where(valid, s, -jnp.inf), jnp.maximum)), jnp.max)

    def pivot(lo, hi):
        mid = 0.5 * lo + 0.5 * hi
        return jnp.where(mid < hi, jnp.maximum(mid, lo), lo)

    def coarse_body(_, carry):
        lo, hi, chi = carry
        v = pivot(lo, hi).astype(BF16).astype(F32)
        vb = jnp.concatenate([v, v], axis=0).astype(BF16)
        one = jnp.ones((2 * SUBLANES, tq), BF16)
        zero = jnp.zeros((2 * SUBLANES, tq), BF16)

        def body(c, accs):
            accs = list(accs)
            for r in range(ck // (2 * SUBLANES)):
                x = scb_ref[c, r * 2 * SUBLANES:(r + 1) * 2 * SUBLANES, :]
                accs[r % 4] = accs[r % 4] + jnp.where(x > vb, one, zero)
            return tuple(accs)

        a0, a1, a2, a3 = lax.fori_loop(0, nk, body, (zero, zero, zero, zero))
        cnt = jnp.sum(((a0 + a1) + (a2 + a3)).astype(F32), axis=0, keepdims=True)
        inside = jnp.logical_and(v >= lo, v < hi)
        up = jnp.logical_and(inside, cnt >= keff)
        down = jnp.logical_and(inside, cnt < keff)
        return jnp.where(up, v, lo), jnp.where(down, v, hi), jnp.where(down, cnt, chi)

    lo, hi, chi = lax.fori_loop(0, COARSE_STEPS, coarse_body, (lo, hi, jnp.zeros((SUBLANES, tq), F32)))

    def count_body(_, carry):
        lo, hi, chi = carry
        v = pivot(lo, hi)

        def body(c, accs):
            accs = list(accs)
            for r in range(ck // SUBLANES):
                x = sc_ref[c, r * SUBLANES:(r + 1) * SUBLANES, :]
                accs[r % 4] = accs[r % 4] + jnp.where(x > v, 1.0, 0.0)
            return tuple(accs)

        zero = jnp.zeros((SUBLANES, tq), F32)
        a0, a1, a2, a3 = lax.fori_loop(0, nk, body, (zero, zero, zero, zero))
        cnt = _rep8((a0 + a1) + (a2 + a3), jnp.sum)
        ge = cnt >= keff
        return jnp.where(ge, v, lo), jnp.where(ge, hi, v), jnp.where(ge, chi, cnt)

    lo, hi, chi = lax.fori_loop(0, COUNT_STEPS, count_body, (lo, hi, chi))

    def tight_cond(state):
        it, pending = state[0], state[1]
        return jnp.logical_and(pending > 0, it < MAX_TIGHTEN_STEPS)

    def tight_body(state):
        it, _, lo, hi, chi = state
        v = pivot(lo, hi)

        def body(c, accs):
            cnt, below, above = accs
            for r in range(ck // SUBLANES):
                x = sc_ref[c, r * SUBLANES:(r + 1) * SUBLANES, :]
                gt = x > v
                cnt = cnt + jnp.where(gt, 1.0, 0.0)
                below = jnp.maximum(below, jnp.where(gt, -jnp.inf, x))
                above = jnp.minimum(above, jnp.where(gt, x, jnp.inf))
            return cnt, below, above

        cnt, below, above = lax.fori_loop(0, nk, body, (jnp.zeros((SUBLANES, tq), F32), -inf8, inf8))
        cnt = _rep8(cnt, jnp.sum)
        below = _rep8(below, jnp.max)
        above = _rep8(above, jnp.min)
        ge = cnt >= keff
        lo = jnp.where(ge, above, lo)
        chi = jnp.where(ge, chi, cnt)
        hi = jnp.where(ge, hi, below)
        pending = jnp.max(jnp.where(lo < hi, 1.0, 0.0)) > 0.5
        return it + 1, pending.astype(jnp.int32), lo, hi, chi

    state = lax.while_loop(tight_cond, tight_body, (jnp.int32(0), jnp.int32(1), lo, hi, chi))
    thr = state[3][0:1, :]
    ties_wanted = (keff - state[4])[0:1, :]

    for hh in range(ATT_HEADS):
        qa_ref[:, hh * tq:(hh + 1) * tq] = q_ref[0, hh]
    acc_ref[...] = jnp.zeros(acc_ref.shape, F32)

    def logits(c, buf, ties_seen):
        x = sc_ref[c]
        eq = x == thr
        prefix = _dot(tri_ref[...], jnp.where(eq, 1.0, 0.0).astype(BF16))
        sel = jnp.logical_or(x > thr, jnp.logical_and(eq, prefix <= ties_wanted - ties_seen))
        bias = jnp.where(sel, 0.0, NEG_BIG).astype(BF16)
        off = pl.multiple_of(c * ck, ck)
        sb = _dot(k_ref[0, pl.ds(off, ck), :], qa_ref[...]).astype(BF16)
        buf[...] = sb + jnp.concatenate([bias] * ATT_HEADS, axis=1)
        return ties_seen + prefix[ck - 1:ck, :]

    def accumulate(c, buf, m_old):
        sb = buf[...]
        cm = jnp.max(_group_reduce(sb, jnp.maximum, 2 * SUBLANES).astype(F32), axis=0, keepdims=True)
        m_new = jnp.maximum(m_old, cm)
        p = jnp.exp2(sb - m_new.astype(BF16))
        acc_ref[...] = acc_ref[...] * jnp.exp2(m_old - m_new) + _dot(v_ref[0, c], p)
        return m_new

    def att_pair(t, carry):
        m, ties = carry
        ties = logits(2 * t + 1, lg_b, ties)
        m = accumulate(2 * t, lg_a, m)
        ties = logits(2 * t + 2, lg_a, ties)
        return accumulate(2 * t + 1, lg_b, m), ties

    def att_tail(_, carry):
        m, ties = carry
        ties = logits(i, lg_b, ties)
        return accumulate(i, lg_b, m), ties

    ties = logits(0, lg_a, jnp.zeros((1, tq), F32))
    m_run = jnp.full((1, ATT_HEADS * tq), NEG_BIG, BF16).astype(F32)
    m_run, ties = lax.fori_loop(0, i // 2, att_pair, (m_run, ties))
    m_run = accumulate(2 * (i // 2), lg_a, m_run)
    lax.fori_loop(0, i % 2, att_tail, (m_run, ties))
    outs = []
    for hh in range(ATT_HEADS):
        a = acc_ref[:, hh * tq:(hh + 1) * tq]
        outs.append(a[0:HEAD_DIM, :] * (1.0 / a[HEAD_DIM:HEAD_DIM + 1, :]))
    out_ref[0] = jnp.concatenate(outs, axis=0).T.astype(BF16)


def _sparse_attention(qt, k_aug, vt_ext, iqt, ik, iwt):
    bsz, _, _, s_len = qt.shape
    tq = ATT_TILE
    topk = min(TOPK_MAX, s_len // 4)
    n_chunks = s_len // tq
    idx = np.arange(tq)
    tri = jnp.asarray(idx[None, :] <= idx[:, None], BF16)
    assert n_chunks * tq // (2 * SUBLANES) <= 256
    heads_t = lambda n, d: pl.BlockSpec((1, n, d, tq), lambda b, j: (b, 0, 0, j))
    return pl.pallas_call(
        functools.partial(_dsa_kernel, topk=topk),
        grid=(bsz, n_chunks),
        in_specs=[heads_t(IDX_HEADS, IDX_DIM), pl.BlockSpec((1, IDX_HEADS, tq), lambda b, j: (b, 0, j)),
                  heads_t(ATT_HEADS, LANES),
                  pl.BlockSpec((1, s_len, IDX_DIM), lambda b, j: (b, 0, 0)),
                  pl.BlockSpec((1, s_len, LANES), lambda b, j: (b, 0, 0)),
                  pl.BlockSpec((1, n_chunks, LANES, tq), lambda b, j: (b, 0, 0, 0)),
                  _const_spec(tri)],
        out_specs=pl.BlockSpec((1, tq, ATT_HEADS * HEAD_DIM), lambda b, j: (b, j, 0)),
        out_shape=jax.ShapeDtypeStruct((bsz, s_len, ATT_HEADS * HEAD_DIM), BF16),
        scratch_shapes=[
            pltpu.VMEM((n_chunks, tq, tq), F32),
            pltpu.VMEM((n_chunks, tq, tq), BF16),
            pltpu.VMEM((IDX_DIM, IDX_HEADS * tq), BF16),
            pltpu.VMEM((LANES, ATT_HEADS * tq), BF16),
            pltpu.VMEM((LANES, ATT_HEADS * tq), F32),
            pltpu.VMEM((tq, ATT_HEADS * tq), BF16),
            pltpu.VMEM((tq, ATT_HEADS * tq), BF16),
        ],
        compiler_params=pltpu.CompilerParams(dimension_semantics=("arbitrary", "arbitrary"),
                                             vmem_limit_bytes=VMEM_LIMIT),
        name="sparse_attention",
    )(iqt, iwt, qt, ik, k_aug, vt_ext, tri)


def _mix_ffn_kernel(x_ref, ya_ref, yb_ref, wo_ref, fg_ref, wup_ref, wd_ref, cw_ref, cb_ref,
                    out_ref, h_ref, act_ref, carry_ref):
    tq = x_ref.shape[1]
    half = ya_ref.shape[2]

    @pl.when(pl.program_id(1) == 0)
    def _():
        carry_ref[...] = jnp.zeros(carry_ref.shape, F32)

    for r0 in range(0, tq, PROLOGUE_ROWS):
        rows = slice(r0, r0 + PROLOGUE_ROWS)
        x1 = (x_ref[0, rows, :] + _dot(ya_ref[0, rows, :], wo_ref[0:half, :])
              + _dot(yb_ref[0, rows, :], wo_ref[half:2 * half, :]))
        out_ref[0, rows, :] = x1
        h_ref[rows, :] = _rmsnorm_rows(x1, fg_ref[...]).astype(BF16)
    first_row = lax.broadcasted_iota(jnp.int32, (SUBLANES, FF_CHUNK), 0) == 0

    def shift_down(a, slot):
        prev = carry_ref[slot]
        carry_ref[slot] = a[tq - SUBLANES:tq, :]
        r = pltpu.roll(a, 1, 0)
        head = jnp.where(first_row, pltpu.roll(prev, 1, 0), r[0:SUBLANES, :])
        return jnp.concatenate([head, r[SUBLANES:, :]], axis=0)

    def conv(cols, slot):
        u = _dot(h_ref[...], wup_ref[:, cols])
        cw = cw_ref[:, cols]
        inner = cw[1:2, :] * u + shift_down(cw[0:1, :] * u, 2 * slot)
        return cw[2:3, :] * u + shift_down(inner, 2 * slot + 1) + cb_ref[:, cols]

    for c in range(D_FF // FF_CHUNK):
        gate = conv(slice(c * FF_CHUNK, (c + 1) * FF_CHUNK), 2 * c)
        up = conv(slice(D_FF + c * FF_CHUNK, D_FF + (c + 1) * FF_CHUNK), 2 * c + 1)
        act_ref[:, c * FF_CHUNK:(c + 1) * FF_CHUNK] = (gate * jax.nn.sigmoid(gate) * up).astype(BF16)
    out_ref[0] += _dot(act_ref[...], wd_ref[...])


def _mix_ffn(x, ya, yb, w_out, ffn_g, w_up, conv_w, conv_b, w_down):
    bsz, s_len, _ = x.shape
    tq = min(FFN_TILE, s_len)
    half = ya.shape[2]
    n_ff = D_FF // FF_CHUNK
    consts = [w_out.astype(BF16), ffn_g.reshape(1, D_MODEL), w_up.astype(BF16), w_down.astype(BF16),
              conv_w, conv_b.reshape(1, -1)]
    tok = lambda width: pl.BlockSpec((1, tq, width), lambda b, j: (b, j, 0))
    return pl.pallas_call(
        _mix_ffn_kernel,
        grid=(bsz, s_len // tq),
        in_specs=[tok(D_MODEL), tok(half), tok(half)] + [_const_spec(c) for c in consts],
        out_specs=tok(D_MODEL),
        out_shape=jax.ShapeDtypeStruct((bsz, s_len, D_MODEL), F32),
        scratch_shapes=[
            pltpu.VMEM((tq, D_MODEL), BF16),
            pltpu.VMEM((tq, D_FF), BF16),
            pltpu.VMEM((4 * n_ff, SUBLANES, FF_CHUNK), F32),
        ],
        compiler_params=pltpu.CompilerParams(dimension_semantics=("arbitrary", "arbitrary"),
                                             vmem_limit_bytes=VMEM_LIMIT),
        name="mix_out_ffn",
    )(x, ya, yb, *consts)


def _odd_in_kernel(x_ref, g_ref, w_ref, lng_ref, lnb_ref, sgw_ref, sgb_ref, wgate_ref, bgate_ref, onorm_ref,
                   tril_ref, yc_ref, yd_ref, state_ref, obuf):
    tq = x_ref.shape[1]
    hk = GLA_HEADS * GLA_DK

    @pl.when(pl.program_id(1) == 0)
    def _():
        state_ref[...] = jnp.zeros(state_ref.shape, F32)

    h = _rmsnorm_rows(x_ref[0], g_ref[...]).astype(BF16)

    c_q = 2 * SG_WIDTH
    c_v = c_q + 2 * hk
    c_r = c_v + GLA_HEADS * GLA_DV
    c_g = c_r + GLA_HEADS * GLA_DV
    pg = _dot(h, w_ref[:, c_g:c_g + LANES])
    qf = _dot(h, w_ref[:, c_q:c_q + hk]) * (GLA_DK ** -0.5)
    kf = _dot(h, w_ref[:, c_q + hk:c_v])
    vf = _dot(h, w_ref[:, c_v:c_r]).astype(BF16)
    rr = _dot(h, w_ref[:, c_r:c_g])
    uv = _dot(h, w_ref[:, 0:c_q])
    g = jax.nn.log_sigmoid(_dot(pg.astype(BF16), wgate_ref[...]) + bgate_ref[...]) * (1.0 / GATE_NORMALIZER)
    g_hi, g_lo = _split_bf16(g)
    b_parts, tot_parts = [], []
    for c in range(tq // GLA_CHUNK):
        rows = slice(c * GLA_CHUNK, (c + 1) * GLA_CHUNK)
        bc = _dot(tril_ref[...], g_hi[rows, :]) + _dot(tril_ref[...], g_lo[rows, :])
        b_parts.append(bc)
        tot_parts.append(jnp.broadcast_to(bc[GLA_CHUNK - 1:GLA_CHUNK, :], bc.shape))
    b = jnp.concatenate(b_parts, axis=0)
    gtot = jnp.concatenate(tot_parts, axis=0)
    qp = qf * jnp.exp(b)
    kp = (kf * jnp.exp(-b)).astype(BF16)
    kpp = kf * jnp.exp(gtot - b)
    dec = jnp.exp(gtot)

    gel = 0.5 * uv * (1.0 + lax.erf(uv * np.float32(np.sqrt(0.5))))
    u = gel[:, 0:SG_WIDTH]
    vv = gel[:, SG_WIDTH:2 * SG_WIDTH]
    mu = jnp.mean(vv, axis=-1, keepdims=True)
    xc = vv - mu
    vn = (xc * lax.rsqrt(jnp.mean(xc * xc, axis=-1, keepdims=True) + LN_EPS) * lng_ref[...] + lnb_ref[...])
    vn = vn.astype(BF16)
    low = lax.broadcasted_iota(jnp.int32, (SG_CHUNK, LANES), 1) < SG_GROUP_DIM
    zero = jnp.zeros((SG_CHUNK, LANES), BF16)
    for cc in range(tq // SG_CHUNK):
        rows = slice(cc * SG_CHUNK, (cc + 1) * SG_CHUNK)
        for p in range(SG_WIDTH // LANES):
            cols = slice(p * LANES, (p + 1) * LANES)
            tile = vn[rows, cols]
            rhs = jnp.concatenate([jnp.where(low, tile, zero), jnp.where(low, zero, tile)], axis=0)
            sp = _dot(sgw_ref[p], rhs) + sgb_ref[:, cols]
            yc_ref[0, rows, cols] = (u[rows, cols] * sp).astype(BF16)

    lane_head = lax.broadcasted_iota(jnp.int32, (GLA_CHUNK, hk), 1) // GLA_DK
    rr_i = lax.broadcasted_iota(jnp.int32, (GLA_HEADS * GLA_CHUNK, GLA_CHUNK), 0) % GLA_CHUNK
    cc_i = lax.broadcasted_iota(jnp.int32, (GLA_HEADS * GLA_CHUNK, GLA_CHUNK), 1)
    causal = rr_i >= cc_i
    chunk_rows = [slice(c * GLA_CHUNK, (c + 1) * GLA_CHUNK) for c in range(tq // GLA_CHUNK)]
    head_rows = [slice(hh * GLA_DK, (hh + 1) * GLA_DK) for hh in range(GLA_HEADS)]
    head_cols = [slice(hh * GLA_DV, (hh + 1) * GLA_DV) for hh in range(GLA_HEADS)]
    lhs_all, a_all, kv_all, dcol_all = [], [], [], []
    for rows in chunk_rows:
        qc = qp[rows, :]
        lhs = jnp.concatenate([jnp.where(lane_head == hh, qc, 0.0) for hh in range(GLA_HEADS)], axis=0).astype(BF16)
        lhs_all.append(lhs)
        a_all.append(jnp.where(causal, _dot_nt(lhs, kp[rows, :]), 0.0).astype(BF16))
    for rows in chunk_rows:
        kt = kpp[rows, :].T.astype(BF16)
        kv_all.append(jnp.concatenate([_dot(kt[hr, :], vf[rows, hc]) for hr, hc in zip(head_rows, head_cols)], axis=0))
        dcol_all.append(dec[rows, :].T[:, 0:1])
    states = [state_ref[...]]
    for kvd, dcol in zip(kv_all, dcol_all):
        states.append(states[-1] * dcol + kvd)
    state_ref[...] = states[-1]
    for rows, lhs, a, state in zip(chunk_rows, lhs_all, a_all, states):
        o_inter = _dot(lhs, state.astype(BF16))
        for hr, hc in zip(head_rows, head_cols):
            obuf[rows, hc] = o_inter[hr, :] + _dot(a[hr, :], vf[rows, hc])

    for hh in range(GLA_HEADS):
        hc = slice(hh * GLA_DV, (hh + 1) * GLA_DV)
        o = _rmsnorm_rows(obuf[:, hc], onorm_ref[...])
        r = rr[:, hc]
        yd_ref[0, :, hc] = (o * (r * jax.nn.sigmoid(r))).astype(BF16)


def _odd_in_proj(x, norm_g, w_in, ln_g, ln_b, sg_w, sg_b, w_gate, b_gate, o_norm):
    bsz, s_len, _ = x.shape
    tq = min(TOK_TILE, s_len)
    hk = GLA_HEADS * GLA_DK
    c_glr = 2 * SG_WIDTH + 2 * hk + GLA_HEADS * GLA_DV
    w = jnp.concatenate([w_in[:, :c_glr], w_in[:, c_glr + GLA_RANK:], w_in[:, c_glr:c_glr + GLA_RANK]], axis=1)
    w = jnp.pad(w, ((0, 0), (0, ODD_PAD - w.shape[1]))).astype(BF16)
    wm = jnp.tril(sg_w)
    sgw = wm.reshape(SG_GROUPS // 2, 2, SG_CHUNK, SG_CHUNK).transpose(0, 2, 1, 3)
    sgw = sgw.reshape(SG_GROUPS // 2, SG_CHUNK, 2 * SG_CHUNK).astype(BF16)
    sgb = jnp.repeat(sg_b.T, SG_GROUP_DIM, axis=1)
    wgate = jnp.pad(w_gate, ((0, LANES - GLA_RANK), (0, 0))).astype(BF16)
    idx = np.arange(GLA_CHUNK)
    tril = jnp.asarray(idx[None, :] <= idx[:, None], BF16)
    consts = [norm_g.reshape(1, D_MODEL), w, ln_g.reshape(1, -1), ln_b.reshape(1, -1), sgw, sgb, wgate,
              b_gate.reshape(1, -1), o_norm.reshape(1, -1), tril]
    tok = lambda width: pl.BlockSpec((1, tq, width), lambda b, j: (b, j, 0))
    return pl.pallas_call(
        _odd_in_kernel,
        grid=(bsz, s_len // tq),
        in_specs=[tok(D_MODEL)] + [_const_spec(c) for c in consts],
        out_specs=[tok(SG_WIDTH), tok(GLA_HEADS * GLA_DV)],
        out_shape=[jax.ShapeDtypeStruct((bsz, s_len, SG_WIDTH), BF16),
                   jax.ShapeDtypeStruct((bsz, s_len, GLA_HEADS * GLA_DV), BF16)],
        scratch_shapes=[pltpu.VMEM((hk, GLA_DV), F32), pltpu.VMEM((tq, GLA_HEADS * GLA_DV), F32)],
        compiler_params=pltpu.CompilerParams(dimension_semantics=("arbitrary", "arbitrary"),
                                             vmem_limit_bytes=VMEM_LIMIT),
        name="odd_in_proj",
    )(x, *consts)


def kernel(x, positions, ev_norm, ev_w_in, sc_conv_w, q_norm, k_norm, ev_w_out, od_norm, od_w_in, sg_ln_g, sg_ln_b,
           sg_w, sg_b, gla_w_gate, gla_b_gate, gla_o_norm, od_w_out, ffn_norm, ffn_w_up, ffn_conv_w, ffn_conv_b,
           ffn_w_down):
    depth = ffn_norm.shape[0]
    cos, sin = _rope_tables(positions)
    for layer in range(depth):
        i = layer // 2
        if layer % 2 == 0:
            ya, q, k, v, iq, ik, iw = _even_in_proj(x, cos, sin, ev_norm[i], ev_w_in[i], sc_conv_w[i], q_norm[i],
                                                    k_norm[i])
            yb = _sparse_attention(q, k, v, iq, ik, iw)
            w_out = ev_w_out[i]
        else:
            ya, yb = _odd_in_proj(x, od_norm[i], od_w_in[i], sg_ln_g[i], sg_ln_b[i], sg_w[i], sg_b[i],
                                  gla_w_gate[i], gla_b_gate[i], gla_o_norm[i])
            w_out = od_w_out[i]
        x = _mix_ffn(x, ya, yb, w_out, ffn_norm[layer], ffn_w_up[layer], ffn_conv_w[layer], ffn_conv_b[layer],
                     ffn_w_down[layer])
    return x
```

```python
import functools

import numpy as np
import jax
import jax.numpy as jnp
from jax import lax
from jax.experimental import pallas as pl
from jax.experimental.pallas import tpu as pltpu

F32 = jnp.float32
BF16 = jnp.bfloat16

D_MODEL = 1024
SC_WIDTH = 512
CONV_WIDTH = 3
ATT_HEADS = 8
HEAD_DIM = 64
IDX_HEADS = 4
IDX_DIM = 64
TOPK_MAX = 256
ROT_DIM = HEAD_DIM // 4
ROT_HALF = ROT_DIM // 2
ROPE_THETA = 500000.0
IDX_SCALE = (IDX_HEADS * IDX_DIM) ** -0.5
ATT_SCALE = HEAD_DIM ** -0.5
SG_GROUPS = 8
SG_GROUP_DIM = 64
SG_WIDTH = 512
SG_CHUNK = 128
GLA_HEADS = 4
GLA_DK = 64
GLA_DV = 128
GLA_RANK = 16
GLA_CHUNK = 64
GATE_NORMALIZER = 16.0
D_FF = 2816
EPS = 1e-6
LN_EPS = 1e-5

EVEN_COLS = 2500
EVEN_PAD = 2560
ODD_PAD = 2688

LANES = 128
SUBLANES = 8
VMEM_LIMIT = 56 * 1024 * 1024

TOK_TILE = 512
FFN_TILE = 1024
ATT_TILE = 256
COARSE_STEPS = 12
COUNT_STEPS = 2
MAX_TIGHTEN_STEPS = 512
FF_CHUNK = 256
PROLOGUE_ROWS = 256
NEG_BIG = -1e30
LOG2E = 1.4426950408889634


def _const_spec(arr):
    nd = arr.ndim
    return pl.BlockSpec(arr.shape, lambda *_: (0,) * nd, pipeline_mode=pl.Buffered(1))


def _dot(a, b):
    return jnp.dot(a, b, preferred_element_type=F32)


def _dot_nt(a, b):
    return lax.dot_general(a, b, (((1,), (1,)), ((), ())), preferred_element_type=F32)


def _split_bf16(a):
    hi = a.astype(BF16)
    lo = (a - hi.astype(F32)).astype(BF16)
    return hi, lo


def _rmsnorm_rows(x, g):
    return x * lax.rsqrt(jnp.mean(x * x, axis=-1, keepdims=True) + EPS) * g


def _rope_kernel(invf_ref, pos_ref, cos_ref, sin_ref):
    pos = pos_ref[...]
    for f in range(ROT_HALF):
        ang = pos * invf_ref[f]
        cos_ref[f] = jnp.cos(ang)
        sin_ref[f] = jnp.sin(ang)


def _rope_tables(positions):
    bsz, s_len = positions.shape
    inv_freq = ROPE_THETA ** (-jnp.arange(0, ROT_DIM, 2, dtype=F32) / ROT_DIM)
    cos, sin = pl.pallas_call(
        _rope_kernel,
        out_shape=[jax.ShapeDtypeStruct((ROT_HALF, bsz, s_len), F32)] * 2,
        in_specs=[pl.BlockSpec(memory_space=pltpu.SMEM), pl.BlockSpec(memory_space=pltpu.VMEM)],
        out_specs=[pl.BlockSpec(memory_space=pltpu.VMEM)] * 2,
        name="rope_tables",
    )(inv_freq, positions.astype(F32))
    return cos.transpose(1, 0, 2), sin.transpose(1, 0, 2)


def _rope_rows(xt, cos, sin):
    parts = []
    for r0 in range(0, xt.shape[0], HEAD_DIM):
        x1 = xt[r0:r0 + ROT_HALF, :]
        x2 = xt[r0 + ROT_HALF:r0 + ROT_DIM, :]
        parts += [x1 * cos - x2 * sin, x1 * sin + x2 * cos, xt[r0 + ROT_DIM:r0 + HEAD_DIM, :]]
    return jnp.concatenate(parts, axis=0)


def _rms_norm_rows_t(xt, gain_col):
    parts = []
    for r0 in range(0, xt.shape[0], HEAD_DIM):
        xh = xt[r0:r0 + HEAD_DIM, :]
        ms = jnp.sum(xh * xh, axis=0, keepdims=True) * (1.0 / HEAD_DIM)
        parts.append(xh * lax.rsqrt(ms + EPS) * gain_col)
    return jnp.concatenate(parts, axis=0)


def _even_in_kernel(x_ref, g_ref, w_ref, cw_ref, qg_ref, kg_ref, cos_ref, sin_ref,
                    ya_ref, q_ref, k_ref, v_ref, iq_ref, ik_ref, iw_ref, zbuf):
    tq = x_ref.shape[1]

    @pl.when(pl.program_id(1) == 0)
    def _():
        zbuf[0:SUBLANES, :] = jnp.zeros((SUBLANES, SC_WIDTH), F32)

    h = _rmsnorm_rows(x_ref[0], g_ref[...]).astype(BF16)

    c_q = 3 * SC_WIDTH
    c_kv = c_q + ATT_HEADS * HEAD_DIM
    c_iq = c_kv + LANES
    c_ik = c_iq + IDX_HEADS * IDX_DIM
    pq = _dot(h, w_ref[:, c_q:c_kv])
    pkv = _dot(h, w_ref[:, c_kv:c_iq])
    piq = _dot(h, w_ref[:, c_iq:c_ik])
    pik = _dot(h, w_ref[:, c_ik:c_ik + LANES])
    p_sc = _dot(h, w_ref[:, 0:c_q])

    bg = p_sc[:, 0:SC_WIDTH]
    z = p_sc[:, SC_WIDTH:2 * SC_WIDTH] * p_sc[:, 2 * SC_WIDTH:3 * SC_WIDTH]
    zbuf[SUBLANES:SUBLANES + tq, :] = z
    cw = cw_ref[...]
    conv = (cw[2:3, :] * z + cw[1:2, :] * zbuf[SUBLANES - 1:SUBLANES - 1 + tq, :]
            + cw[0:1, :] * zbuf[SUBLANES - 2:SUBLANES - 2 + tq, :])
    ya_ref[0] = (bg * conv).astype(BF16)
    zbuf[0:SUBLANES, :] = zbuf[tq:tq + SUBLANES, :]

    cos = cos_ref[0]
    sin = sin_ref[0]
    gain_q = jnp.concatenate([qg_ref[...]] * (tq // LANES), axis=1)
    gain_k = jnp.concatenate([kg_ref[...]] * (tq // LANES), axis=1)

    qt = (_rope_rows(_rms_norm_rows_t(pq.T, gain_q), cos, sin) * (ATT_SCALE * LOG2E)).astype(BF16)
    pad = jnp.zeros((LANES - HEAD_DIM, tq), BF16)
    for hh in range(ATT_HEADS):
        q_ref[0, hh, 0:HEAD_DIM, :] = qt[hh * HEAD_DIM:(hh + 1) * HEAD_DIM, :]
        q_ref[0, hh, HEAD_DIM:LANES, :] = pad

    kvt = pkv.T
    kt = _rope_rows(_rms_norm_rows_t(kvt[0:HEAD_DIM, :], gain_k), cos, sin)
    k_ref[0] = jnp.concatenate([kt, jnp.zeros((LANES - HEAD_DIM, tq), F32)], axis=0).T.astype(BF16)
    vt = jnp.concatenate([kvt[HEAD_DIM:LANES, :], jnp.ones((LANES - HEAD_DIM, tq), F32)], axis=0).astype(BF16)
    for cc in range(tq // ATT_TILE):
        v_ref[0, cc] = vt[:, cc * ATT_TILE:(cc + 1) * ATT_TILE]

    iqt = _rope_rows(piq.T, cos, sin).astype(BF16)
    for hh in range(IDX_HEADS):
        iq_ref[0, hh] = iqt[hh * IDX_DIM:(hh + 1) * IDX_DIM, :]

    ikt = pik.T
    ik_rot = jnp.concatenate([_rope_rows(ikt[0:IDX_DIM, :], cos, sin), ikt[IDX_DIM:LANES, :]], axis=0)
    ik_ref[0] = ik_rot.T[:, 0:IDX_DIM].astype(BF16)
    iw_ref[0] = ikt[IDX_DIM:IDX_DIM + IDX_HEADS, :] * IDX_SCALE


def _even_in_proj(x, cos, sin, norm_g, w_in, conv_w, q_g, k_g):
    bsz, s_len, _ = x.shape
    tq = min(TOK_TILE, s_len)
    w = jnp.pad(w_in, ((0, 0), (0, EVEN_PAD - EVEN_COLS))).astype(BF16)
    as_columns = lambda g: jnp.broadcast_to(g[:, None], (HEAD_DIM, LANES))
    consts = [norm_g.reshape(1, D_MODEL), w, conv_w, as_columns(q_g), as_columns(k_g)]
    tok = lambda width: pl.BlockSpec((1, tq, width), lambda b, j: (b, j, 0))
    rope = pl.BlockSpec((1, ROT_HALF, tq), lambda b, j: (b, 0, j))
    per_tile = tq // ATT_TILE
    heads_t = lambda n, d: pl.BlockSpec((1, n, d, tq), lambda b, j: (b, 0, 0, j))
    return pl.pallas_call(
        _even_in_kernel,
        grid=(bsz, s_len // tq),
        in_specs=[tok(D_MODEL)] + [_const_spec(c) for c in consts] + [rope, rope],
        out_specs=[tok(SC_WIDTH), heads_t(ATT_HEADS, LANES), tok(LANES),
                   pl.BlockSpec((1, per_tile, LANES, ATT_TILE), lambda b, j: (b, j, 0, 0)),
                   heads_t(IDX_HEADS, IDX_DIM), tok(IDX_DIM),
                   pl.BlockSpec((1, IDX_HEADS, tq), lambda b, j: (b, 0, j))],
        out_shape=[
            jax.ShapeDtypeStruct((bsz, s_len, SC_WIDTH), BF16),
            jax.ShapeDtypeStruct((bsz, ATT_HEADS, LANES, s_len), BF16),
            jax.ShapeDtypeStruct((bsz, s_len, LANES), BF16),
            jax.ShapeDtypeStruct((bsz, s_len // ATT_TILE, LANES, ATT_TILE), BF16),
            jax.ShapeDtypeStruct((bsz, IDX_HEADS, IDX_DIM, s_len), BF16),
            jax.ShapeDtypeStruct((bsz, s_len, IDX_DIM), BF16),
            jax.ShapeDtypeStruct((bsz, IDX_HEADS, s_len), F32),
        ],
        scratch_shapes=[pltpu.VMEM((tq + SUBLANES, SC_WIDTH), F32)],
        compiler_params=pltpu.CompilerParams(dimension_semantics=("arbitrary", "arbitrary"),
                                             vmem_limit_bytes=VMEM_LIMIT),
        name="even_in_proj",
    )(x, *consts, cos, sin)


def _group_reduce(x, op, group=SUBLANES):
    parts = [x[r:r + group, :] for r in range(0, x.shape[0], group)]
    while len(parts) > 1:
        nxt = [op(parts[a], parts[a + 1]) for a in range(0, len(parts) - 1, 2)]
        if len(parts) % 2:
            nxt.append(parts[-1])
        parts = nxt
    return parts[0]


def _rep8(x8, op):
    return jnp.broadcast_to(op(x8, axis=0, keepdims=True), x8.shape)


def _ceil_bf16(x):
    bits = lax.bitcast_convert_type(x, jnp.int32)
    bits = jnp.where(bits >= 0, bits + 0xFFFF, bits)
    return lax.bitcast_convert_type(bits & jnp.int32(-65536), F32).astype(BF16)


def _dsa_kernel(iq_ref, iw_ref, q_ref, ik_ref, k_ref, v_ref, tri_ref, out_ref,
                sc_ref, scb_ref, iqa_ref, qa_ref, acc_ref, lg_a, lg_b, *, topk):
    tq = ATT_TILE
    ck = ATT_TILE
    i = pl.program_id(1)
    nk = i + 1
    qpos = i * tq + lax.broadcasted_iota(jnp.int32, (SUBLANES, tq), 1)
    keff = jnp.minimum(qpos + 1, topk).astype(F32)
    iw = iw_ref[0]
    inf8 = jnp.full((SUBLANES, tq), jnp.inf, F32)

    for hh in range(IDX_HEADS):
        iqa_ref[:, hh * tq:(hh + 1) * tq] = iq_ref[0, hh]

    def scores(c):
        off = pl.multiple_of(c * ck, ck)
        z = _dot(ik_ref[0, pl.ds(off, ck), :], iqa_ref[...])
        acc = jnp.zeros((ck, tq), F32)
        for hh in range(IDX_HEADS):
            acc = acc + jnp.maximum(z[:, hh * tq:(hh + 1) * tq], 0.0) * iw[hh:hh + 1, :]
        return acc

    def score_chunk(c, carry):
        lo, hi = carry
        s = scores(c)
        sc_ref[c] = s
        scb_ref[c] = _ceil_bf16(s)
        return jnp.minimum(lo, _group_reduce(s, jnp.minimum)), jnp.maximum(hi, _group_reduce(s, jnp.maximum))

    lo, hi = lax.fori_loop(0, i // 2, lambda t, cr: score_chunk(2 * t + 1, score_chunk(2 * t, cr)), (inf8, -inf8))
    lo, hi = lax.fori_loop(0, i % 2, lambda _, cr: score_chunk(i - 1, cr), (lo, hi))
    s = scores(i)
    kpos = i * ck + lax.broadcasted_iota(jnp.int32, (ck, tq), 0)
    valid = kpos <= i * tq + lax.broadcasted_iota(jnp.int32, (ck, tq), 1)
    sc_ref[i] = jnp.where(valid, s, -jnp.inf)
    scb_ref[i] = _ceil_bf16(jnp.where(valid, s, -jnp.inf))
    lo = _rep8(jnp.minimum(lo, _group_reduce(jnp.where(valid, s, jnp.inf), jnp.minimum)), jnp.min)
    hi = _rep8(jnp.maximum(hi, _group_reduce(jnp.where(valid, s, -jnp.inf), jnp.maximum)), jnp.max)

    def pivot(lo, hi):
        mid = 0.5 * lo + 0.5 * hi
        return jnp.where(mid < hi, jnp.maximum(mid, lo), lo)

    def coarse_body(_, carry):
        lo, hi, chi = carry
        v = pivot(lo, hi).astype(BF16).astype(F32)
        vb = jnp.concatenate([v, v], axis=0).astype(BF16)
        one = jnp.ones((2 * SUBLANES, tq), BF16)
        zero = jnp.zeros((2 * SUBLANES, tq), BF16)

        def body(c, accs):
            accs = list(accs)
            for r in range(ck // (2 * SUBLANES)):
                x = scb_ref[c, r * 2 * SUBLANES:(r + 1) * 2 * SUBLANES, :]
                accs[r % 4] = accs[r % 4] + jnp.where(x > vb, one, zero)
            return tuple(accs)

        a0, a1, a2, a3 = lax.fori_loop(0, nk, body, (zero, zero, zero, zero))
        cnt = jnp.sum(((a0 + a1) + (a2 + a3)).astype(F32), axis=0, keepdims=True)
        inside = jnp.logical_and(v >= lo, v < hi)
        up = jnp.logical_and(inside, cnt >= keff)
        down = jnp.logical_and(inside, cnt < keff)
        return jnp.where(up, v, lo), jnp.where(down, v, hi), jnp.where(down, cnt, chi)

    lo, hi, chi = lax.fori_loop(0, COARSE_STEPS, coarse_body, (lo, hi, jnp.zeros((SUBLANES, tq), F32)))

    def count_body(_, carry):
        lo, hi, chi = carry
        v = pivot(lo, hi)

        def body(c, accs):
            accs = list(accs)
            for r in range(ck // SUBLANES):
                x = sc_ref[c, r * SUBLANES:(r + 1) * SUBLANES, :]
                accs[r % 4] = accs[r % 4] + jnp.where(x > v, 1.0, 0.0)
            return tuple(accs)

        zero = jnp.zeros((SUBLANES, tq), F32)
        a0, a1, a2, a3 = lax.fori_loop(0, nk, body, (zero, zero, zero, zero))
        cnt = _rep8((a0 + a1) + (a2 + a3), jnp.sum)
        ge = cnt >= keff
        return jnp.where(ge, v, lo), jnp.where(ge, hi, v), jnp.where(ge, chi, cnt)

    lo, hi, chi = lax.fori_loop(0, COUNT_STEPS, count_body, (lo, hi, chi))

    def tight_cond(state):
        it, pending = state[0], state[1]
        return jnp.logical_and(pending > 0, it < MAX_TIGHTEN_STEPS)

    def tight_body(state):
        it, _, lo, hi, chi = state
        v = pivot(lo, hi)

        def body(c, accs):
            cnt, below, above = accs
            for r in range(ck // SUBLANES):
                x = sc_ref[c, r * SUBLANES:(r + 1) * SUBLANES, :]
                gt = x > v
                cnt = cnt + jnp.where(gt, 1.0, 0.0)
                below = jnp.maximum(below, jnp.where(gt, -jnp.inf, x))
                above = jnp.minimum(above, jnp.where(gt, x, jnp.inf))
            return cnt, below, above

        cnt, below, above = lax.fori_loop(0, nk, body, (jnp.zeros((SUBLANES, tq), F32), -inf8, inf8))
        cnt = _rep8(cnt, jnp.sum)
        below = _rep8(below, jnp.max)
        above = _rep8(above, jnp.min)
        ge = cnt >= keff
        lo = jnp.where(ge, above, lo)
        chi = jnp.where(ge, chi, cnt)
        hi = jnp.where(ge, hi, below)
        pending = jnp.max(jnp.where(lo < hi, 1.0, 0.0)) > 0.5
        return it + 1, pending.astype(jnp.int32), lo, hi, chi

    state = lax.while_loop(tight_cond, tight_body, (jnp.int32(0), jnp.int32(1), lo, hi, chi))
    thr = state[3][0:1, :]
    ties_wanted = (keff - state[4])[0:1, :]

    for hh in range(ATT_HEADS):
        qa_ref[:, hh * tq:(hh + 1) * tq] = q_ref[0, hh]
    acc_ref[...] = jnp.zeros(acc_ref.shape, F32)

    def logits(c, buf, ties_seen):
        x = sc_ref[c]
        eq = x == thr
        prefix = _dot(tri_ref[...], jnp.where(eq, 1.0, 0.0).astype(BF16))
        sel = jnp.logical_or(x > thr, jnp.logical_and(eq, prefix <= ties_wanted - ties_seen))
        bias = jnp.where(sel, 0.0, NEG_BIG).astype(BF16)
        off = pl.multiple_of(c * ck, ck)
        sb = _dot(k_ref[0, pl.ds(off, ck), :], qa_ref[...]).astype(BF16)
        buf[...] = sb + jnp.concatenate([bias] * ATT_HEADS, axis=1)
        return ties_seen + prefix[ck - 1:ck, :]

    def accumulate(c, buf, m_old):
        sb = buf[...]
        cm = jnp.max(_group_reduce(sb, jnp.maximum, 2 * SUBLANES).astype(F32), axis=0, keepdims=True)
        m_new = jnp.maximum(m_old, cm)
        p = jnp.exp2(sb - m_new.astype(BF16))
        acc_ref[...] = acc_ref[...] * jnp.exp2(m_old - m_new) + _dot(v_ref[0, c], p)
        return m_new

    def att_pair(t, carry):
        m, ties = carry
        ties = logits(2 * t + 1, lg_b, ties)
        m = accumulate(2 * t, lg_a, m)
        ties = logits(2 * t + 2, lg_a, ties)
        return accumulate(2 * t + 1, lg_b, m), ties

    def att_tail(_, carry):
        m, ties = carry
        ties = logits(i, lg_b, ties)
        return accumulate(i, lg_b, m), ties

    ties = logits(0, lg_a, jnp.zeros((1, tq), F32))
    m_run = jnp.full((1, ATT_HEADS * tq), NEG_BIG, BF16).astype(F32)
    m_run, ties = lax.fori_loop(0, i // 2, att_pair, (m_run, ties))
    m_run = accumulate(2 * (i // 2), lg_a, m_run)
    lax.fori_loop(0, i % 2, att_tail, (m_run, ties))
    outs = []
    for hh in range(ATT_HEADS):
        a = acc_ref[:, hh * tq:(hh + 1) * tq]
        outs.append(a[0:HEAD_DIM, :] * (1.0 / a[HEAD_DIM:HEAD_DIM + 1, :]))
    out_ref[0] = jnp.concatenate(outs, axis=0).T.astype(BF16)


def _sparse_attention(qt, k_aug, vt_ext, iqt, ik, iwt):
    bsz, _, _, s_len = qt.shape
    tq = ATT_TILE
    topk = min(TOPK_MAX, s_len // 4)
    n_chunks = s_len // tq
    idx = np.arange(tq)
    tri = jnp.asarray(idx[None, :] <= idx[:, None], BF16)
    assert n_chunks * tq // (2 * SUBLANES) <= 256
    heads_t = lambda n, d: pl.BlockSpec((1, n, d, tq), lambda b, j: (b, 0, 0, j))
    return pl.pallas_call(
        functools.partial(_dsa_kernel, topk=topk),
        grid=(bsz, n_chunks),
        in_specs=[heads_t(IDX_HEADS, IDX_DIM), pl.BlockSpec((1, IDX_HEADS, tq), lambda b, j: (b, 0, j)),
                  heads_t(ATT_HEADS, LANES),
                  pl.BlockSpec((1, s_len, IDX_DIM), lambda b, j: (b, 0, 0)),
                  pl.BlockSpec((1, s_len, LANES), lambda b, j: (b, 0, 0)),
                  pl.BlockSpec((1, n_chunks, LANES, tq), lambda b, j: (b, 0, 0, 0)),
                  _const_spec(tri)],
        out_specs=pl.BlockSpec((1, tq, ATT_HEADS * HEAD_DIM), lambda b, j: (b, j, 0)),
        out_shape=jax.ShapeDtypeStruct((bsz, s_len, ATT_HEADS * HEAD_DIM), BF16),
        scratch_shapes=[
            pltpu.VMEM((n_chunks, tq, tq), F32),
            pltpu.VMEM((n_chunks, tq, tq), BF16),
            pltpu.VMEM((IDX_DIM, IDX_HEADS * tq), BF16),
            pltpu.VMEM((LANES, ATT_HEADS * tq), BF16),
            pltpu.VMEM((LANES, ATT_HEADS * tq), F32),
            pltpu.VMEM((tq, ATT_HEADS * tq), BF16),
            pltpu.VMEM((tq, ATT_HEADS * tq), BF16),
        ],
        compiler_params=pltpu.CompilerParams(dimension_semantics=("arbitrary", "arbitrary"),
                                             vmem_limit_bytes=VMEM_LIMIT),
        name="sparse_attention",
    )(iqt, iwt, qt, ik, k_aug, vt_ext, tri)


def _mix_ffn_kernel(x_ref, ya_ref, yb_ref, wo_ref, fg_ref, wup_ref, wd_ref, cw_ref, cb_ref,
                    out_ref, h_ref, act_ref, carry_ref):
    tq = x_ref.shape[1]
    half = ya_ref.shape[2]

    @pl.when(pl.program_id(1) == 0)
    def _():
        carry_ref[...] = jnp.zeros(carry_ref.shape, F32)

    for r0 in range(0, tq, PROLOGUE_ROWS):
        rows = slice(r0, r0 + PROLOGUE_ROWS)
        x1 = (x_ref[0, rows, :] + _dot(ya_ref[0, rows, :], wo_ref[0:half, :])
              + _dot(yb_ref[0, rows, :], wo_ref[half:2 * half, :]))
        out_ref[0, rows, :] = x1
        h_ref[rows, :] = _rmsnorm_rows(x1, fg_ref[...]).astype(BF16)
    first_row = lax.broadcasted_iota(jnp.int32, (SUBLANES, FF_CHUNK), 0) == 0

    def shift_down(a, slot):
        prev = carry_ref[slot]
        carry_ref[slot] = a[tq - SUBLANES:tq, :]
        r = pltpu.roll(a, 1, 0)
        head = jnp.where(first_row, pltpu.roll(prev, 1, 0), r[0:SUBLANES, :])
        return jnp.concatenate([head, r[SUBLANES:, :]], axis=0)

    def conv(cols, slot):
        u = _dot(h_ref[...], wup_ref[:, cols])
        cw = cw_ref[:, cols]
        inner = cw[1:2, :] * u + shift_down(cw[0:1, :] * u, 2 * slot)
        return cw[2:3, :] * u + shift_down(inner, 2 * slot + 1) + cb_ref[:, cols]

    for c in range(D_FF // FF_CHUNK):
        gate = conv(slice(c * FF_CHUNK, (c + 1) * FF_CHUNK), 2 * c)
        up = conv(slice(D_FF + c * FF_CHUNK, D_FF + (c + 1) * FF_CHUNK), 2 * c + 1)
        act_ref[:, c * FF_CHUNK:(c + 1) * FF_CHUNK] = (gate * jax.nn.sigmoid(gate) * up).astype(BF16)
    out_ref[0] += _dot(act_ref[...], wd_ref[...])


def _mix_ffn(x, ya, yb, w_out, ffn_g, w_up, conv_w, conv_b, w_down):
    bsz, s_len, _ = x.shape
    tq = min(FFN_TILE, s_len)
    half = ya.shape[2]
    n_ff = D_FF // FF_CHUNK
    consts = [w_out.astype(BF16), ffn_g.reshape(1, D_MODEL), w_up.astype(BF16), w_down.astype(BF16),
              conv_w, conv_b.reshape(1, -1)]
    tok = lambda width: pl.BlockSpec((1, tq, width), lambda b, j: (b, j, 0))
    return pl.pallas_call(
        _mix_ffn_kernel,
        grid=(bsz, s_len // tq),
        in_specs=[tok(D_MODEL), tok(half), tok(half)] + [_const_spec(c) for c in consts],
        out_specs=tok(D_MODEL),
        out_shape=jax.ShapeDtypeStruct((bsz, s_len, D_MODEL), F32),
        scratch_shapes=[
            pltpu.VMEM((tq, D_MODEL), BF16),
            pltpu.VMEM((tq, D_FF), BF16),
            pltpu.VMEM((4 * n_ff, SUBLANES, FF_CHUNK), F32),
        ],
        compiler_params=pltpu.CompilerParams(dimension_semantics=("arbitrary", "arbitrary"),
                                             vmem_limit_bytes=VMEM_LIMIT),
        name="mix_out_ffn",
    )(x, ya, yb, *consts)


def _odd_in_kernel(x_ref, g_ref, w_ref, lng_ref, lnb_ref, sgw_ref, sgb_ref, wgate_ref, bgate_ref, onorm_ref,
                   tril_ref, yc_ref, yd_ref, state_ref, obuf):
    tq = x_ref.shape[1]
    hk = GLA_HEADS * GLA_DK

    @pl.when(pl.program_id(1) == 0)
    def _():
        state_ref[...] = jnp.zeros(state_ref.shape, F32)

    h = _rmsnorm_rows(x_ref[0], g_ref[...]).astype(BF16)

    c_q = 2 * SG_WIDTH
    c_v = c_q + 2 * hk
    c_r = c_v + GLA_HEADS * GLA_DV
    c_g = c_r + GLA_HEADS * GLA_DV
    pg = _dot(h, w_ref[:, c_g:c_g + LANES])
    qf = _dot(h, w_ref[:, c_q:c_q + hk]) * (GLA_DK ** -0.5)
    kf = _dot(h, w_ref[:, c_q + hk:c_v])
    vf = _dot(h, w_ref[:, c_v:c_r]).astype(BF16)
    rr = _dot(h, w_ref[:, c_r:c_g])
    uv = _dot(h, w_ref[:, 0:c_q])
    g = jax.nn.log_sigmoid(_dot(pg.astype(BF16), wgate_ref[...]) + bgate_ref[...]) * (1.0 / GATE_NORMALIZER)
    g_hi, g_lo = _split_bf16(g)
    b_parts, tot_parts = [], []
    for c in range(tq // GLA_CHUNK):
        rows = slice(c * GLA_CHUNK, (c + 1) * GLA_CHUNK)
        bc = _dot(tril_ref[...], g_hi[rows, :]) + _dot(tril_ref[...], g_lo[rows, :])
        b_parts.append(bc)
        tot_parts.append(jnp.broadcast_to(bc[GLA_CHUNK - 1:GLA_CHUNK, :], bc.shape))
    b = jnp.concatenate(b_parts, axis=0)
    gtot = jnp.concatenate(tot_parts, axis=0)
    qp = qf * jnp.exp(b)
    kp = (kf * jnp.exp(-b)).astype(BF16)
    kpp = kf * jnp.exp(gtot - b)
    dec = jnp.exp(gtot)

    gel = 0.5 * uv * (1.0 + lax.erf(uv * np.float32(np.sqrt(0.5))))
    u = gel[:, 0:SG_WIDTH]
    vv = gel[:, SG_WIDTH:2 * SG_WIDTH]
    mu = jnp.mean(vv, axis=-1, keepdims=True)
    xc = vv - mu
    vn = (xc * lax.rsqrt(jnp.mean(xc * xc, axis=-1, keepdims=True) + LN_EPS) * lng_ref[...] + lnb_ref[...])
    vn = vn.astype(BF16)
    low = lax.broadcasted_iota(jnp.int32, (SG_CHUNK, LANES), 1) < SG_GROUP_DIM
    zero = jnp.zeros((SG_CHUNK, LANES), BF16)
    for cc in range(tq // SG_CHUNK):
        rows = slice(cc * SG_CHUNK, (cc + 1) * SG_CHUNK)
        for p in range(SG_WIDTH // LANES):
            cols = slice(p * LANES, (p + 1) * LANES)
            tile = vn[rows, cols]
            rhs = jnp.concatenate([jnp.where(low, tile, zero), jnp.where(low, zero, tile)], axis=0)
            sp = _dot(sgw_ref[p], rhs) + sgb_ref[:, cols]
            yc_ref[0, rows, cols] = (u[rows, cols] * sp).astype(BF16)

    lane_head = lax.broadcasted_iota(jnp.int32, (GLA_CHUNK, hk), 1) // GLA_DK
    rr_i = lax.broadcasted_iota(jnp.int32, (GLA_HEADS * GLA_CHUNK, GLA_CHUNK), 0) % GLA_CHUNK
    cc_i = lax.broadcasted_iota(jnp.int32, (GLA_HEADS * GLA_CHUNK, GLA_CHUNK), 1)
    causal = rr_i >= cc_i
    chunk_rows = [slice(c * GLA_CHUNK, (c + 1) * GLA_CHUNK) for c in range(tq // GLA_CHUNK)]
    head_rows = [slice(hh * GLA_DK, (hh + 1) * GLA_DK) for hh in range(GLA_HEADS)]
    head_cols = [slice(hh * GLA_DV, (hh + 1) * GLA_DV) for hh in range(GLA_HEADS)]
    lhs_all, a_all, kv_all, dcol_all = [], [], [], []
    for rows in chunk_rows:
        qc = qp[rows, :]
        lhs = jnp.concatenate([jnp.where(lane_head == hh, qc, 0.0) for hh in range(GLA_HEADS)], axis=0).astype(BF16)
        lhs_all.append(lhs)
        a_all.append(jnp.where(causal, _dot_nt(lhs, kp[rows, :]), 0.0).astype(BF16))
    for rows in chunk_rows:
        kt = kpp[rows, :].T.astype(BF16)
        kv_all.append(jnp.concatenate([_dot(kt[hr, :], vf[rows, hc]) for hr, hc in zip(head_rows, head_cols)], axis=0))
        dcol_all.append(dec[rows, :].T[:, 0:1])
    states = [state_ref[...]]
    for kvd, dcol in zip(kv_all, dcol_all):
        states.append(states[-1] * dcol + kvd)
    state_ref[...] = states[-1]
    for rows, lhs, a, state in zip(chunk_rows, lhs_all, a_all, states):
        o_inter = _dot(lhs, state.astype(BF16))
        for hr, hc in zip(head_rows, head_cols):
            obuf[rows, hc] = o_inter[hr, :] + _dot(a[hr, :], vf[rows, hc])

    for hh in range(GLA_HEADS):
        hc = slice(hh * GLA_DV, (hh + 1) * GLA_DV)
        o = _rmsnorm_rows(obuf[:, hc], onorm_ref[...])
        r = rr[:, hc]
        yd_ref[0, :, hc] = (o * (r * jax.nn.sigmoid(r))).astype(BF16)


def _odd_in_proj(x, norm_g, w_in, ln_g, ln_b, sg_w, sg_b, w_gate, b_gate, o_norm):
    bsz, s_len, _ = x.shape
    tq = min(TOK_TILE, s_len)
    hk = GLA_HEADS * GLA_DK
    c_glr = 2 * SG_WIDTH + 2 * hk + GLA_HEADS * GLA_DV
    w = jnp.concatenate([w_in[:, :c_glr], w_in[:, c_glr + GLA_RANK:], w_in[:, c_glr:c_glr + GLA_RANK]], axis=1)
    w = jnp.pad(w, ((0, 0), (0, ODD_PAD - w.shape[1]))).astype(BF16)
    wm = jnp.tril(sg_w)
    sgw = wm.reshape(SG_GROUPS // 2, 2, SG_CHUNK, SG_CHUNK).transpose(0, 2, 1, 3)
    sgw = sgw.reshape(SG_GROUPS // 2, SG_CHUNK, 2 * SG_CHUNK).astype(BF16)
    sgb = jnp.repeat(sg_b.T, SG_GROUP_DIM, axis=1)
    wgate = jnp.pad(w_gate, ((0, LANES - GLA_RANK), (0, 0))).astype(BF16)
    idx = np.arange(GLA_CHUNK)
    tril = jnp.asarray(idx[None, :] <= idx[:, None], BF16)
    consts = [norm_g.reshape(1, D_MODEL), w, ln_g.reshape(1, -1), ln_b.reshape(1, -1), sgw, sgb, wgate,
              b_gate.reshape(1, -1), o_norm.reshape(1, -1), tril]
    tok = lambda width: pl.BlockSpec((1, tq, width), lambda b, j: (b, j, 0))
    return pl.pallas_call(
        _odd_in_kernel,
        grid=(bsz, s_len // tq),
        in_specs=[tok(D_MODEL)] + [_const_spec(c) for c in consts],
        out_specs=[tok(SG_WIDTH), tok(GLA_HEADS * GLA_DV)],
        out_shape=[jax.ShapeDtypeStruct((bsz, s_len, SG_WIDTH), BF16),
                   jax.ShapeDtypeStruct((bsz, s_len, GLA_HEADS * GLA_DV), BF16)],
        scratch_shapes=[pltpu.VMEM((hk, GLA_DV), F32), pltpu.VMEM((tq, GLA_HEADS * GLA_DV), F32)],
        compiler_params=pltpu.CompilerParams(dimension_semantics=("arbitrary", "arbitrary"),
                                             vmem_limit_bytes=VMEM_LIMIT),
        name="odd_in_proj",
    )(x, *consts)


def kernel(x, positions, ev_norm, ev_w_in, sc_conv_w, q_norm, k_norm, ev_w_out, od_norm, od_w_in, sg_ln_g, sg_ln_b,
           sg_w, sg_b, gla_w_gate, gla_b_gate, gla_o_norm, od_w_out, ffn_norm, ffn_w_up, ffn_conv_w, ffn_conv_b,
           ffn_w_down):
    depth = ffn_norm.shape[0]
    cos, sin = _rope_tables(positions)
    for layer in range(depth):
        i = layer // 2
        if layer % 2 == 0:
            ya, q, k, v, iq, ik, iw = _even_in_proj(x, cos, sin, ev_norm[i], ev_w_in[i], sc_conv_w[i], q_norm[i],
                                                    k_norm[i])
            yb = _sparse_attention(q, k, v, iq, ik, iw)
            w_out = ev_w_out[i]
        else:
            ya, yb = _odd_in_proj(x, od_norm[i], od_w_in[i], sg_ln_g[i], sg_ln_b[i], sg_w[i], sg_b[i],
                                  gla_w_gate[i], gla_b_gate[i], gla_o_norm[i])
            w_out = od_w_out[i]
        x = _mix_ffn(x, ya, yb, w_out, ffn_norm[layer], ffn_w_up[layer], ffn_conv_w[layer], ffn_conv_b[layer],
                     ffn_w_down[layer])
    return x
```
